```python
import jax, jax.numpy as jnp
from jax import lax
import numpy as np

D_MODEL = 1024
BATCH = 4
SEQ = 8192
DEPTH = 2

CTX_LEN = 256
GRID_W = 64
N_DIR = 2
N_EVEN = (DEPTH + 1) // 2
N_ODD = DEPTH // 2
D_FF = 2816
MIX_W = 1024
EPS = 1e-6

RWKV_HEADS = 8
RWKV_HD = 64
RWKV_W = RWKV_HEADS * RWKV_HD
DECAY_LORA = 64
AAA_LORA = 64
GATE_LORA = 128
RWKV_GN_EPS = 64e-5
RWKV_SHIFT_W = 3 * RWKV_W + DECAY_LORA + AAA_LORA

MLSTM_HEADS = 4
MLSTM_HD = 128
MLSTM_W = MLSTM_HEADS * MLSTM_HD
MLSTM_CHUNK = 128

HGRN_HEADS = 4
HGRN_FD = 128
HGRN_ID = 128
HGRN_W = HGRN_HEADS * HGRN_FD
HGRN_CHUNK = 64

LRU_W = 512
LRU_BLOCKS = 8
LRU_BD = LRU_W // LRU_BLOCKS
CONV_W = 4
CONV_PAD_L = 2
LRU_C = 8.0

EVEN_SPLITS = (RWKV_W, RWKV_W, RWKV_W, N_DIR * DECAY_LORA, N_DIR * AAA_LORA, GATE_LORA,
               MLSTM_W, MLSTM_W, MLSTM_W, MLSTM_W, N_DIR * MLSTM_HEADS, N_DIR * MLSTM_HEADS)
ODD_SPLITS = (HGRN_W, N_DIR * HGRN_W, HGRN_W, HGRN_W, LRU_W, LRU_W)
EVEN_IN = sum(EVEN_SPLITS)
ODD_IN = sum(ODD_SPLITS)

kernel_name = 'hybrid_bidir_rwkv7_mlstm_hgrn2_rglru_dit'


def split_cols(t, sizes):
    idx = np.cumsum(sizes)[:-1].tolist()
    return jnp.split(t, idx, axis=-1)


def rmsnorm(x, g):
    xf = x.astype(jnp.float32)
    y = xf * lax.rsqrt(jnp.mean(xf * xf, axis=-1, keepdims=True) + EPS)
    return (y * g.astype(jnp.float32)).astype(x.dtype)


def frames_shared(t):
    return jnp.stack([t, jnp.flip(t, axis=1)])


def frames_dir(t):
    return jnp.stack([t[0], jnp.flip(t[1], axis=1)])


def merge_frames(o):
    return o[0] + jnp.flip(o[1], axis=1)


def shift_prev(t):
    return jnp.pad(t, ((0, 0), (0, 0), (1, 0), (0, 0)))[:, :, :-1]


def chunk_major(t, T):
    d, b, l, h = t.shape[:4]
    t = t.reshape((d, b, l // T, T, h) + t.shape[4:])
    return jnp.moveaxis(t, (2, 4), (0, 3))


def unchunk(t):
    t = jnp.moveaxis(t, (0, 3), (2, 4))
    d, b, nc, T, h = t.shape[:5]
    return t.reshape((d, b, nc * T, h) + t.shape[5:])


def to_col_major(t, rows):
    b, l, f = t.shape
    return t.reshape(b, rows, GRID_W, f).transpose(0, 2, 1, 3).reshape(b, l, f)


def to_row_major(t, rows):
    b, l, f = t.shape
    return t.reshape(b, GRID_W, rows, f).transpose(0, 2, 1, 3).reshape(b, l, f)


def swiglu(t, w1, w3, w2):
    return (jax.nn.silu(t @ w1) * (t @ w3)) @ w2


def bidir(seq, p, cols_ctx, cols_lat, state0):
    y_ctx, state = seq(cols_ctx, state0, p)
    y_lat, _ = seq(cols_lat, state, p)
    return y_ctx, y_lat


def rwkv7_seq(cols, S0, p):
    r, k, v, wd, ad, gd = (t.astype(jnp.float32) for t in cols)
    B_, L_ = r.shape[:2]
    heads = lambda t: t.reshape(t.shape[:3] + (RWKV_HEADS, RWKV_HD))
    per_dir = lambda t, n: frames_dir(jnp.moveaxis(t.reshape(B_, L_, N_DIR, n), 2, 0))
    s = jnp.concatenate([frames_shared(r), frames_shared(k), frames_shared(v),
                         per_dir(wd, DECAY_LORA), per_dir(ad, AAA_LORA)], axis=-1)
    s = s + p['mu'][:, None, None, :] * (shift_prev(s) - s)
    r, k, v, wd, ad = split_cols(s, (RWKV_W, RWKV_W, RWKV_W, DECAY_LORA, AAA_LORA))
    w = -jax.nn.softplus(-(p['w0'][:, None, None, :]
                           + jnp.einsum('dblr,drc->dblc', jnp.tanh(wd), p['w2']))) - 0.5
    a = jax.nn.sigmoid(p['a0'][:, None, None, :] + jnp.einsum('dblr,drc->dblc', ad, p['a2']))
    kk = heads(k * p['k_k'][:, None, None, :])
    kk = kk / jnp.maximum(jnp.sqrt(jnp.sum(kk * kk, axis=-1, keepdims=True)), 1e-12)
    k = heads(k * (1.0 + (a - 1.0) * p['k_a'][:, None, None, :]))
    r, v, a = heads(r), heads(v), heads(a)
    decay = heads(jnp.exp(-jnp.exp(w)))
    xs = tuple(jnp.moveaxis(t, 2, 0) for t in (r, decay, k, v, kk, kk * a))

    def step(S, inp):
        r_t, w_t, k_t, v_t, kk_t, b_t = inp
        S = (S * w_t[..., None, :]
             - jnp.einsum('...ij,...j->...i', S, kk_t)[..., :, None] * b_t[..., None, :]
             + v_t[..., :, None] * k_t[..., None, :])
        return S, jnp.einsum('...ij,...j->...i', S, r_t)

    S, o = lax.scan(step, S0, xs)
    o = jnp.moveaxis(o, 0, 2)
    mean = jnp.mean(o, axis=-1, keepdims=True)
    var = jnp.mean(jnp.square(o - mean), axis=-1, keepdims=True)
    gn = ((o - mean) * lax.rsqrt(var + RWKV_GN_EPS)).reshape(N_DIR, B_, L_, RWKV_W)
    gn = gn * p['ln_w'][:, None, None, :] + p['ln_b'][:, None, None, :]
    bonus = (jnp.sum(r * k * p['r_k'][:, None, None], axis=-1, keepdims=True) * v).reshape(N_DIR, B_, L_, RWKV_W)
    g = jnp.einsum('blr,rc->blc', jax.nn.sigmoid(gd), p['g2'])
    return merge_frames(gn + bonus) * g, S


def mlstm_seq(cols, state0, p):
    q, k, v, o_pre, i_pre, f_pre = (t.astype(jnp.float32) for t in cols)
    B_, L_ = q.shape[:2]
    T = MLSTM_CHUNK
    heads = lambda t: t.reshape(B_, L_, MLSTM_HEADS, MLSTM_HD)
    gates = lambda t, bias: frames_dir(jnp.moveaxis(t.reshape(B_, L_, N_DIR, MLSTM_HEADS), 2, 0)
                                       + bias[:, None, None, :])
    qc = chunk_major(frames_shared(heads(q)) * MLSTM_HD ** -0.5, T)
    kc = chunk_major(frames_shared(heads(k)), T)
    vc = chunk_major(frames_shared(heads(v)), T)
    li = chunk_major(gates(i_pre, p['b_i'])[..., None], T)[..., 0]
    lf = chunk_major(jax.nn.log_sigmoid(gates(f_pre, p['b_f']))[..., None], T)[..., 0]
    mask = jnp.tril(jnp.ones((T, T), bool))

    def step(carry, inp):
        C0, n0, m0 = carry
        q_c, k_c, v_c, li_c, lf_c = inp
        b = jnp.cumsum(lf_c, axis=-1)
        logD = jnp.where(mask, b[..., :, None] - b[..., None, :] + li_c[..., None, :], -jnp.inf)
        m_prev = b + m0[..., None]
        m = jnp.maximum(m_prev, jnp.max(logD, axis=-1))
        s = jnp.einsum('...tk,...sk->...ts', q_c, k_c) * jnp.exp(logD - m[..., None])
        inter = jnp.exp(m_prev - m)
        num = jnp.einsum('...ts,...sv->...tv', s, v_c) + inter[..., None] * jnp.einsum('...vk,...tk->...tv', C0, q_c)
        den = jnp.sum(s, axis=-1) + inter * jnp.einsum('...k,...tk->...t', n0, q_c)
        h = num / jnp.maximum(jnp.abs(den), jnp.exp(-m))[..., None]
        bT = b[..., -1]
        gsum = bT[..., None] - b + li_c
        m_new = jnp.maximum(bT + m0, jnp.max(gsum, axis=-1))
        wts = jnp.exp(gsum - m_new[..., None])
        carry_decay = jnp.exp(bT + m0 - m_new)
        C = carry_decay[..., None, None] * C0 + jnp.einsum('...t,...tv,...tk->...vk', wts, v_c, k_c)
        n = carry_decay[..., None] * n0 + jnp.einsum('...t,...tk->...k', wts, k_c)
        return (C, n, m_new), h

    state, h = lax.scan(step, state0, (qc, kc, vc, li, lf))
    h = merge_frames(unchunk(h))
    y = rmsnorm(h, p['ng'].reshape(MLSTM_HEADS, MLSTM_HD)).reshape(B_, L_, MLSTM_W)
    return y * jax.nn.sigmoid(o_pre), state


def hgrn2_seq(cols, S0, p):
    q, f_pre, i, g = (t.astype(jnp.float32) for t in cols)
    B_, L_ = q.shape[:2]
    T = HGRN_CHUNK
    lb = p['lb']
    qc = chunk_major(frames_shared(jax.nn.silu(q).reshape(B_, L_, HGRN_HEADS, HGRN_FD)), T)
    vc = chunk_major(frames_shared(i.reshape(B_, L_, HGRN_HEADS, HGRN_ID)), T)
    f = lb + (1.0 - lb) * jax.nn.sigmoid(f_pre.reshape(B_, L_, N_DIR, HGRN_W))
    f = frames_dir(jnp.moveaxis(f, 2, 0)).reshape(N_DIR, B_, L_, HGRN_HEADS, HGRN_FD)
    kc = chunk_major(1.0 - f, T)
    lfc = chunk_major(jnp.log(f), T)
    mask = jnp.tril(jnp.ones((T, T), bool))[..., None]

    def step(S, inp):
        q_c, k_c, v_c, lf_c = inp
        Bc = jnp.cumsum(lf_c, axis=-2)
        rel = jnp.exp(jnp.where(mask, Bc[..., :, None, :] - Bc[..., None, :, :], -jnp.inf))
        A = jnp.einsum('...tf,...tsf,...sf->...ts', q_c, rel, k_c)
        o = A @ v_c + jnp.einsum('...tf,...fi->...ti', q_c * jnp.exp(Bc), S)
        BT = Bc[..., -1:, :]
        S = jnp.exp(BT)[..., 0, :, None] * S + jnp.einsum('...sf,...si->...fi', k_c * jnp.exp(BT - Bc), v_c)
        return S, o

    S, o = lax.scan(step, S0, (qc, kc, vc, lfc))
    o = merge_frames(unchunk(o))
    y = rmsnorm(o, p['ng'].reshape(HGRN_HEADS, HGRN_ID)).reshape(B_, L_, HGRN_W)
    return y * jax.nn.silu(g), S


def rglru_seq(cols, h0, p):
    xb, gb = (t.astype(jnp.float32) for t in cols)
    B_, L_ = xb.shape[:2]
    xp = jnp.pad(xb, ((0, 0), (CONV_PAD_L, CONV_W - 1 - CONV_PAD_L), (0, 0)))
    xconv = sum(xp[:, j:j + L_] * p['conv_w'][j] for j in range(CONV_W)) + p['conv_b']
    xc = frames_shared(xconv)
    xh = xc.reshape(N_DIR, B_, L_, LRU_BLOCKS, LRU_BD)
    blockdiag = lambda w, b: (jnp.einsum('dblhi,dhij->dblhj', xh, w).reshape(N_DIR, B_, L_, LRU_W)
                              + b[:, None, None, :])
    r = jax.nn.sigmoid(blockdiag(p['wa'], p['ba']))
    ig = jax.nn.sigmoid(blockdiag(p['wi'], p['bi']))
    log_a = -LRU_C * r * jax.nn.softplus(-p['lam'])[:, None, None, :]
    a = jnp.exp(log_a)
    u = jnp.sqrt(-jnp.expm1(2.0 * log_a)) * (ig * xc)
    u = u.at[:, :, 0].add(a[:, :, 0] * h0)

    def combine(e1, e2):
        a1, b1 = e1
        a2, b2 = e2
        return a1 * a2, a2 * b1 + b2

    _, h = lax.associative_scan(combine, (a, u), axis=2)
    return merge_frames(h) * jax.nn.gelu(gb), h[:, :, -1]


def setup_inputs(seed: int = 0) -> dict:
    key = jax.random.key(seed)
    ks = list(jax.random.split(key, 48))
    f32 = jnp.float32

    def nrm(shape, scale):
        return jax.random.normal(ks.pop(), shape, f32) * scale

    def uni(shape, lo, hi):
        return jax.random.uniform(ks.pop(), shape, f32, lo, hi)

    D = D_MODEL
    ramp = (jnp.arange(RWKV_W, dtype=f32) / (RWKV_W - 1)) ** 0.9
    a_base = uni((N_ODD, N_DIR, LRU_W), 0.9, 0.999) ** (1.0 / LRU_C)
    return {
        'x': nrm((BATCH, SEQ, D), 1.0),
        'c': nrm((BATCH, D), 1.0),
        'ctx': nrm((BATCH, CTX_LEN, D), 1.0),
        'c_ctx': nrm((D,), 1.0),
        'norm1_g': 1.0 + nrm((DEPTH, D), 0.05),
        'norm2_g': 1.0 + nrm((DEPTH, D), 0.05),
        'mod_w': nrm((DEPTH, D, 6 * D), D ** -0.5),
        'mod_b': nrm((DEPTH, 6 * D), 0.02),
        'ffn_w1': nrm((DEPTH, D, D_FF), D ** -0.5),
        'ffn_w3': nrm((DEPTH, D, D_FF), D ** -0.5),
        'ffn_w2': nrm((DEPTH, D_FF, D), D_FF ** -0.5),
        'final_g': 1.0 + nrm((D,), 0.05),
        'ev_in_w': nrm((N_EVEN, D, EVEN_IN), D ** -0.5),
        'ev_out_w': nrm((N_EVEN, MIX_W, D), MIX_W ** -0.5),
        'rwkv_mu': uni((N_EVEN, N_DIR, RWKV_SHIFT_W), 0.0, 1.0),
        'rwkv_w0': -6.0 + 5.0 * ramp + nrm((N_EVEN, N_DIR, RWKV_W), 0.1),
        'rwkv_w2': nrm((N_EVEN, N_DIR, DECAY_LORA, RWKV_W), 0.5 * DECAY_LORA ** -0.5),
        'rwkv_a0': nrm((N_EVEN, N_DIR, RWKV_W), 0.1),
        'rwkv_a2': nrm((N_EVEN, N_DIR, AAA_LORA, RWKV_W), AAA_LORA ** -0.5),
        'rwkv_kk': 0.85 + nrm((N_EVEN, N_DIR, RWKV_W), 0.05),
        'rwkv_ka': 1.0 + nrm((N_EVEN, N_DIR, RWKV_W), 0.05),
        'rwkv_rk': nrm((N_EVEN, N_DIR, RWKV_HEADS, RWKV_HD), 0.1),
        'rwkv_lnw': 1.0 + nrm((N_EVEN, N_DIR, RWKV_W), 0.05),
        'rwkv_lnb': nrm((N_EVEN, N_DIR, RWKV_W), 0.02),
        'rwkv_g2': nrm((N_EVEN, GATE_LORA, RWKV_W), GATE_LORA ** -0.5),
        'mlstm_bi': nrm((N_EVEN, N_DIR, MLSTM_HEADS), 0.1),
        'mlstm_bf': jnp.linspace(3.0, 6.0, MLSTM_HEADS, dtype=f32) + nrm((N_EVEN, N_DIR, MLSTM_HEADS), 0.1),
        'mlstm_ng': 1.0 + nrm((N_EVEN, MLSTM_W), 0.05),
        'od_in_w': nrm((N_ODD, D, ODD_IN), D ** -0.5),
        'od_out_w': nrm((N_ODD, MIX_W, D), MIX_W ** -0.5),
        'hgrn_lb': 1.0 + nrm((DEPTH, HGRN_W), 0.1),
        'hgrn_ng': 1.0 + nrm((N_ODD, HGRN_W), 0.05),
        'lru_conv_w': nrm((N_ODD, CONV_W, LRU_W), CONV_W ** -0.5),
        'lru_conv_b': nrm((N_ODD, LRU_W), 0.02),
        'lru_wa': nrm((N_ODD, N_DIR, LRU_BLOCKS, LRU_BD, LRU_BD), LRU_BD ** -0.5),
        'lru_ba': nrm((N_ODD, N_DIR, LRU_W), 0.1),
        'lru_wi': nrm((N_ODD, N_DIR, LRU_BLOCKS, LRU_BD, LRU_BD), LRU_BD ** -0.5),
        'lru_bi': nrm((N_ODD, N_DIR, LRU_W), 0.1),
        'lru_lam': jnp.log(a_base) - jnp.log1p(-a_base),
    }


def reference(x, c, ctx, c_ctx, norm1_g, norm2_g, mod_w, mod_b, ffn_w1, ffn_w3, ffn_w2, final_g,
              ev_in_w, ev_out_w, rwkv_mu, rwkv_w0, rwkv_w2, rwkv_a0, rwkv_a2, rwkv_kk, rwkv_ka,
              rwkv_rk, rwkv_lnw, rwkv_lnb, rwkv_g2, mlstm_bi, mlstm_bf, mlstm_ng,
              od_in_w, od_out_w, hgrn_lb, hgrn_ng, lru_conv_w, lru_conv_b, lru_wa, lru_ba,
              lru_wi, lru_bi, lru_lam):
    B_, L_, _ = x.shape
    rows = L_ // GRID_W
    f32 = jnp.float32
    sm = jax.nn.softmax(hgrn_lb.astype(f32), axis=0)
    lower_bounds = jnp.cumsum(sm, axis=0) - sm[0]
    h, hc = x, ctx
    for l in range(DEPTH):
        j = l // 2
        last = l == DEPTH - 1
        mod = jnp.split((jax.nn.silu(c) @ mod_w[l] + mod_b[l])[:, None, :], 6, axis=-1)
        mod_c = jnp.split(jax.nn.silu(c_ctx) @ mod_w[l] + mod_b[l], 6, axis=-1)
        u = rmsnorm(h, norm1_g[l]) * (1.0 + mod[1]) + mod[0]
        uc = rmsnorm(hc, norm1_g[l]) * (1.0 + mod_c[1]) + mod_c[0]
        if l % 2 == 0:
            cols = split_cols(u @ ev_in_w[j], EVEN_SPLITS)
            cols_c = split_cols(uc @ ev_in_w[j], EVEN_SPLITS)
            pa = {'mu': rwkv_mu[j], 'w0': rwkv_w0[j], 'w2': rwkv_w2[j], 'a0': rwkv_a0[j],
                  'a2': rwkv_a2[j], 'k_k': rwkv_kk[j], 'k_a': rwkv_ka[j], 'r_k': rwkv_rk[j],
                  'ln_w': rwkv_lnw[j], 'ln_b': rwkv_lnb[j], 'g2': rwkv_g2[j]}
            pb = {'b_i': mlstm_bi[j], 'b_f': mlstm_bf[j], 'ng': mlstm_ng[j]}
            s_a = jnp.zeros((N_DIR, B_, RWKV_HEADS, RWKV_HD, RWKV_HD), f32)
            s_b = (jnp.zeros((N_DIR, B_, MLSTM_HEADS, MLSTM_HD, MLSTM_HD), f32),
                   jnp.zeros((N_DIR, B_, MLSTM_HEADS, MLSTM_HD), f32),
                   jnp.zeros((N_DIR, B_, MLSTM_HEADS), f32))
            y1_c, y1_l = bidir(rwkv7_seq, pa, cols_c[:6], cols[:6], s_a)
            y2_c, y2_l = bidir(mlstm_seq, pb, cols_c[6:], cols[6:], s_b)
            mix = jnp.concatenate([y1_l, y2_l], axis=-1).astype(h.dtype) @ ev_out_w[j]
            out_w = ev_out_w[j]
        else:
            cols = split_cols(to_col_major(u, rows) @ od_in_w[j], ODD_SPLITS)
            cols_c = split_cols(uc @ od_in_w[j], ODD_SPLITS)
            pc = {'lb': lower_bounds[l], 'ng': hgrn_ng[j]}
            pd = {'conv_w': lru_conv_w[j], 'conv_b': lru_conv_b[j], 'wa': lru_wa[j], 'ba': lru_ba[j],
                  'wi': lru_wi[j], 'bi': lru_bi[j], 'lam': lru_lam[j]}
            s_c = jnp.zeros((N_DIR, B_, HGRN_HEADS, HGRN_FD, HGRN_ID), f32)
            s_d = jnp.zeros((N_DIR, B_, LRU_W), f32)
            y1_c, y1_l = bidir(hgrn2_seq, pc, cols_c[:4], cols[:4], s_c)
            y2_c, y2_l = bidir(rglru_seq, pd, cols_c[4:], cols[4:], s_d)
            mix = to_row_major(jnp.concatenate([y1_l, y2_l], axis=-1).astype(h.dtype), rows) @ od_out_w[j]
            out_w = od_out_w[j]
        h = h + mod[2] * mix
        v = rmsnorm(h, norm2_g[l]) * (1.0 + mod[4]) + mod[3]
        h = h + mod[5] * swiglu(v, ffn_w1[l], ffn_w3[l], ffn_w2[l])
        if not last:
            mix_c = jnp.concatenate([y1_c, y2_c], axis=-1).astype(hc.dtype) @ out_w
            hc = hc + mod_c[2] * mix_c
            vc = rmsnorm(hc, norm2_g[l]) * (1.0 + mod_c[4]) + mod_c[3]
            hc = hc + mod_c[5] * swiglu(vc, ffn_w1[l], ffn_w3[l], ffn_w2[l])
    return rmsnorm(h, final_g)
```

```python
import functools

import jax
import jax.numpy as jnp
from jax import lax
from jax.experimental import pallas as pl
from jax.experimental.pallas import tpu as pltpu

F32 = jnp.float32
BF16 = jnp.bfloat16

D_MODEL = 1024
GRID_W = 64
D_FF = 2816
EPS = 1e-6

RWKV_HEADS = 8
RWKV_HD = 64
RWKV_W = RWKV_HEADS * RWKV_HD
LORA = 64
GATE_LORA = 128
RWKV_GN_EPS = 64e-5
MLSTM_HEADS = 4
MLSTM_HD = 128
MLSTM_W = MLSTM_HEADS * MLSTM_HD
HGRN_HEADS = 4
HGRN_FD = 128
HGRN_W = HGRN_HEADS * HGRN_FD
LRU_W = 512
LRU_BLOCKS = 8
LRU_BD = LRU_W // LRU_BLOCKS
LRU_C = 8.0

T_RWKV = 64
T_MLSTM = 128
T_HGRN = 64
T_LRU = 256
TM = 256

EV_N = 4096
EV_RKV, EV_Q, EV_K, EV_V, EV_O = 0, 3, 4, 5, 6
EV_LORA, EV_GD, EV_GATES = 28, 30, 31
OD_N = 3584
OD_Q, OD_F, OD_I, OD_G, OD_XB, OD_GB = 0, 1, 3, 4, 5, 6

VMEM_LIMIT = 56 * 1024 * 1024

NT = ((1,), (1,))
NN = ((1,), (0,))
TN = ((0,), (0,))


def _split(x, n):
    parts = []
    r = x
    for i in range(n):
        p = r.astype(BF16)
        parts.append(p)
        if i + 1 < n:
            r = r - p.astype(F32)
    return parts


def _mm(a, b, dims=NN, pa=1, pb=1):
    dn = (dims, ((), ()))
    ap = _split(a, pa) if a.dtype != BF16 else [a]
    bp = _split(b, pb) if b.dtype != BF16 else [b]
    n = max(len(ap), len(bp))
    acc = None
    for i, x in enumerate(ap):
        for j, y in enumerate(bp):
            if i + j < n:
                t = lax.dot_general(x, y, dn, preferred_element_type=F32)
                acc = t if acc is None else acc + t
    return acc


def _sigmoid(x):
    return 1.0 / (1.0 + jnp.exp(-x))


def _silu(x):
    return x * _sigmoid(x)


def _log_sigmoid(x):
    return jnp.minimum(x, 0.0) - jnp.log1p(jnp.exp(-jnp.abs(x)))


def _softplus(x):
    return jnp.maximum(x, 0.0) + jnp.log1p(jnp.exp(-jnp.abs(x)))


def _masks(T, reverse):
    ti = lax.broadcasted_iota(jnp.int32, (T, T), 0)
    si = lax.broadcasted_iota(jnp.int32, (T, T), 1)
    if reverse:
        return si >= ti, si > ti
    return si <= ti, si < ti


def _ones_mask(incl):
    return jnp.where(incl, 1.0, 0.0).astype(BF16)


def _shift_prev(x, carry_row, reverse):
    T = x.shape[0]
    row = lax.broadcasted_iota(jnp.int32, x.shape, 0)
    if reverse:
        return jnp.where(row == T - 1, carry_row, pltpu.roll(x, T - 1, 0))
    return jnp.where(row == 0, carry_row, pltpu.roll(x, 1, 0))


def _fwd_blk(i, n_lat, n_ctx):
    return jnp.where(i < n_ctx, n_lat + i, i - n_ctx)


def _bwd_blk(i, n_lat, n_ctx):
    return n_lat + n_ctx - 1 - i


def _mod_kernel(c_ref, w_ref, b_ref, o_ref):
    c = c_ref[...]
    o_ref[0] = _mm(_silu(c), w_ref[0], NN, 3, 3) + b_ref[0]


def _modulation(c8, mod_w, mod_b):
    depth, d, n = mod_w.shape
    return pl.pallas_call(
        _mod_kernel,
        grid=(depth, n // d),
        in_specs=[
            pl.BlockSpec((8, d), lambda l, j: (0, 0)),
            pl.BlockSpec((1, d, d), lambda l, j: (l, 0, j)),
            pl.BlockSpec((1, 1, d), lambda l, j: (l, 0, j)),
        ],
        out_specs=pl.BlockSpec((1, 8, d), lambda l, j: (l, 0, j)),
        out_shape=jax.ShapeDtypeStruct((depth, 8, n), F32),
        compiler_params=pltpu.CompilerParams(vmem_limit_bytes=VMEM_LIMIT),
        name="adaln_modulation",
    )(c8, mod_w, mod_b.reshape(depth, 1, n))


def _norm_mod(x, g, shift, scale):
    ms = jnp.mean(x * x, axis=-1, keepdims=True)
    return (x * lax.rsqrt(ms + EPS) * g) * (1.0 + scale) + shift


def _in_even_kernel(x_ref, g_ref, mod_ref, w_ref, o_ref):
    m = mod_ref[0]
    u = _norm_mod(x_ref[0], g_ref[...], m[0:1], m[1:2])
    o_ref[0] = jnp.dot(u.astype(BF16), w_ref[...], preferred_element_type=F32)


def _in_proj_even(h_all, g, mod, w, n_lat_tiles):
    b, ltot, d = h_all.shape
    n = w.shape[1]
    nt = ltot // TM
    return pl.pallas_call(
        _in_even_kernel,
        grid=(b, nt),
        in_specs=[
            pl.BlockSpec((1, TM, d), lambda i, t: (i, t, 0)),
            pl.BlockSpec((1, d), lambda i, t: (0, 0)),
            pl.BlockSpec((1, 6, d), lambda i, t: (jnp.where(t >= n_lat_tiles, 4, i), 0, 0)),
            pl.BlockSpec((d, n), lambda i, t: (0, 0)),
        ],
        out_specs=pl.BlockSpec((1, TM, n), lambda i, t: (i, t, 0)),
        out_shape=jax.ShapeDtypeStruct((b, ltot, n), F32),
        compiler_params=pltpu.CompilerParams(vmem_limit_bytes=VMEM_LIMIT),
        name="in_proj_even",
    )(h_all, g, mod, w)


def _in_odd_kernel(xs_ref, xc_ref, g_ref, mod_ref, w_ref, o_ref, *, n_cols):
    t = pl.program_id(1)
    x = jnp.where(t < n_cols, xs_ref[0], xc_ref[0])
    m = mod_ref[0]
    u = _norm_mod(x, g_ref[...], m[0:1], m[1:2])
    o_ref[0] = jnp.dot(u.astype(BF16), w_ref[...], preferred_element_type=F32)


def _in_proj_odd(h_all, g, mod, w, rows):
    b, ltot, d = h_all.shape
    n = w.shape[1]
    lat = rows * GRID_W
    n_ctx_tiles = (ltot - lat) // rows
    h_strided = h_all.reshape(b, ltot // GRID_W, GRID_W * d)
    return pl.pallas_call(
        functools.partial(_in_odd_kernel, n_cols=GRID_W),
        grid=(b, GRID_W + n_ctx_tiles),
        in_specs=[
            pl.BlockSpec((1, rows, d), lambda i, t: (i, 0, jnp.minimum(t, GRID_W - 1))),
            pl.BlockSpec((1, rows, d), lambda i, t: (i, jnp.maximum(t, GRID_W), 0)),
            pl.BlockSpec((1, d), lambda i, t: (0, 0)),
            pl.BlockSpec((1, 6, d), lambda i, t: (jnp.where(t >= GRID_W, 4, i), 0, 0)),
            pl.BlockSpec((d, n), lambda i, t: (0, 0)),
        ],
        out_specs=pl.BlockSpec((1, rows, n), lambda i, t: (i, t, 0)),
        out_shape=jax.ShapeDtypeStruct((b, ltot, n), F32),
        compiler_params=pltpu.CompilerParams(vmem_limit_bytes=VMEM_LIMIT),
        name="in_proj_odd",
    )(h_strided, h_all, g, mod, w)


def _rwkv_head(Ah, Bh, Kh, Rh, Vh, S, gT, incl, strict, p):
    T = Ah.shape[0]
    mm = functools.partial(_mm, pa=p, pb=p)
    AR = jnp.concatenate([Ah, Rh], axis=0)
    BK = jnp.concatenate([Bh, Kh], axis=0)
    G = mm(AR, BK, NT)
    N = jnp.where(strict, G[:T, :T], 0.0)
    Lk = jnp.where(strict, G[:T, T:], 0.0)
    Prb = jnp.where(incl, G[T:, :T], 0.0)
    Prk = jnp.where(incl, G[T:, T:], 0.0)
    SS = mm(AR, S, NT)
    LV = mm(jnp.concatenate([Lk, Prk], axis=0), Vh)
    X = SS[:T] + LV[:T]
    P = -N
    k = 1
    while k < T:
        if 2 * k < T:
            PX = mm(P, jnp.concatenate([P, X], axis=1))
            P = PX[:, :T]
            X = X + PX[:, T:]
        else:
            X = X + mm(P, X)
        k *= 2
    U = X
    O = SS[T:] + LV[T:] - mm(Prb, U)
    dS = mm(jnp.concatenate([Vh, -U], axis=0), jnp.concatenate([Kh, Bh], axis=0), TN)
    return O, (S + dS) * gT


def _rwkv_dir(d, x, lo, cx_ref, cl_ref, s_ref, mu_x, mu_lo, p512, wcomb, ones_blk, reverse, p):
    T = x.shape[0]
    incl, strict = _masks(T, reverse)
    edge = 0 if reverse else T - 1
    xs = x + mu_x * (_shift_prev(x, cx_ref[d, 0:1, :], reverse) - x)
    los = lo + mu_lo * (_shift_prev(lo, cl_ref[d, 0:1, :], reverse) - lo)
    cx_ref[d, 0:1, :] = x[edge:edge + 1, :]
    cl_ref[d, 0:1, :] = lo[edge:edge + 1, :]
    r = xs[:, :RWKV_W]
    k = xs[:, RWKV_W:2 * RWKV_W]
    v = xs[:, 2 * RWKV_W:]
    w0, a0, k_k, k_a, ln_w, ln_b, r_k = (p512[i:i + 1, :] for i in range(7))
    lane = lax.broadcasted_iota(jnp.int32, los.shape, 1)
    z = jnp.where(lane < LORA, jnp.tanh(los), los)
    wa = _mm(z, wcomb, NN, 3, 3)
    w = -_softplus(-(w0 + wa[:, :RWKV_W])) - 0.5
    a = _sigmoid(a0 + wa[:, RWKV_W:])
    logdec = -jnp.exp(w)
    kk = k * k_k
    ss = _mm(kk * kk, ones_blk, NN, 3, 1)
    kk = kk / jnp.maximum(jnp.sqrt(ss), 1e-12)
    k2 = k * (1.0 + (a - 1.0) * k_a)
    b = kk * a
    cl = _mm(_ones_mask(incl), logdec, NN, 1, 3)
    e_pos = jnp.exp(cl)
    e_neg = jnp.exp(-cl)
    A = kk * jnp.exp(cl - logdec)
    Bh = b * e_neg
    Kh = k2 * e_neg
    Rt = r * e_pos
    gT = e_pos[edge:edge + 1, :]
    outs = []
    for h in range(RWKV_HEADS):
        sl = slice(h * RWKV_HD, (h + 1) * RWKV_HD)
        O, S_new = _rwkv_head(A[:, sl], Bh[:, sl], Kh[:, sl], Rt[:, sl], v[:, sl],
                              s_ref[d, h], gT[:, sl], incl, strict, p)
        s_ref[d, h] = S_new
        outs.append(O)
    o = jnp.concatenate(outs, axis=1)
    inv_n = 1.0 / RWKV_HD
    mean = _mm(o, ones_blk, NN, 3, 1) * inv_n
    oc = o - mean
    var = _mm(oc * oc, ones_blk, NN, 3, 1) * inv_n
    gn = oc * lax.rsqrt(var + RWKV_GN_EPS) * ln_w + ln_b
    bonus = _mm(r * k2 * r_k, ones_blk, NN, 3, 1) * v
    return gn + bonus


def _rwkv_kernel(xf_ref, xb_ref, lf_ref, lb_ref, mux_ref, mulo_ref, p512_ref, wcomb_ref, ones_ref,
                 of_ref, ob_ref, s_ref, cx_ref, cl_ref, *, n_ctx, prec):
    i = pl.program_id(1)

    @pl.when(i == 0)
    def _():
        s_ref[...] = jnp.zeros_like(s_ref)

    @pl.when((i == 0) | (i == n_ctx))
    def _():
        cx_ref[...] = jnp.zeros_like(cx_ref)
        cl_ref[...] = jnp.zeros_like(cl_ref)

    ones_blk = ones_ref[...]
    for d, (x_ref, lo_ref, o_ref) in enumerate(((xf_ref, lf_ref, of_ref), (xb_ref, lb_ref, ob_ref))):
        o_ref[0] = _rwkv_dir(d, x_ref[0], lo_ref[0], cx_ref, cl_ref, s_ref, mux_ref[d], mulo_ref[d],
                             p512_ref[d], wcomb_ref[d], ones_blk, bool(d), prec)


def _rwkv(proj, mu_x, mu_lo, p512, wcomb, n_lat, n_ctx, prec=2):
    b, ltot, _ = proj.shape
    T = T_RWKV
    ones_blk = jnp.kron(jnp.eye(RWKV_HEADS, dtype=F32), jnp.ones((RWKV_HD, RWKV_HD), F32)).astype(BF16)
    fwd = functools.partial(_fwd_blk, n_lat=n_lat, n_ctx=n_ctx)
    bwd = functools.partial(_bwd_blk, n_lat=n_lat, n_ctx=n_ctx)
    full = lambda shape: pl.BlockSpec(shape, lambda bi, i: (0,) * len(shape))
    out = jax.ShapeDtypeStruct((b, ltot, RWKV_W), F32)
    return pl.pallas_call(
        functools.partial(_rwkv_kernel, n_ctx=n_ctx, prec=prec),
        grid=(b, n_lat + n_ctx),
        in_specs=[
            pl.BlockSpec((1, T, 3 * RWKV_W), lambda bi, i: (bi, fwd(i), EV_RKV)),
            pl.BlockSpec((1, T, 3 * RWKV_W), lambda bi, i: (bi, bwd(i), EV_RKV)),
            pl.BlockSpec((1, T, 128), lambda bi, i: (bi, fwd(i), EV_LORA)),
            pl.BlockSpec((1, T, 128), lambda bi, i: (bi, bwd(i), EV_LORA + 1)),
            full((2, 1, 3 * RWKV_W)), full((2, 1, 128)), full((2, 8, RWKV_W)),
            full((2, 128, 2 * RWKV_W)), full((RWKV_W, RWKV_W)),
        ],
        out_specs=[
            pl.BlockSpec((1, T, RWKV_W), lambda bi, i: (bi, fwd(i), 0)),
            pl.BlockSpec((1, T, RWKV_W), lambda bi, i: (bi, bwd(i), 0)),
        ],
        out_shape=[out, out],
        scratch_shapes=[
            pltpu.VMEM((2, RWKV_HEADS, RWKV_HD, RWKV_HD), F32),
            pltpu.VMEM((2, 8, 3 * RWKV_W), F32),
            pltpu.VMEM((2, 8, 128), F32),
        ],
        compiler_params=pltpu.CompilerParams(
            dimension_semantics=("arbitrary", "arbitrary"), vmem_limit_bytes=VMEM_LIMIT),
        name="rwkv7_chunked",
    )(proj, proj, proj, proj, mu_x, mu_lo, p512, wcomb, ones_blk)


def _mlstm_dir(d, q, k, v, gates, gbias, c_ref, n_ref, m_ref, reverse, p):
    T = q.shape[0]
    mm = functools.partial(_mm, pa=p, pb=p)
    incl, _ = _masks(T, reverse)
    edge = 0 if reverse else T - 1
    g = gates + gbias
    lane = lax.broadcasted_iota(jnp.int32, g.shape, 1)
    g = jnp.where(lane < 2 * MLSTM_HEADS, g, _log_sigmoid(g))
    gt = g.T
    ones = _ones_mask(incl)
    b_col = _mm(ones, g, NN, 1, 3)
    b_row = _mm(gt, ones, NT, 3, 1)
    outs = []
    for h in range(MLSTM_HEADS):
        sl = slice(h * MLSTM_HD, (h + 1) * MLSTM_HD)
        ci = d * MLSTM_HEADS + h
        cf = 2 * MLSTM_HEADS + ci
        qh = q[:, sl] * (MLSTM_HD ** -0.5)
        kh = k[:, sl]
        vh = v[:, sl]
        bc = b_col[:, cf:cf + 1]
        br = b_row[cf:cf + 1, :]
        li_c = g[:, ci:ci + 1]
        li_r = gt[ci:ci + 1, :]
        st = d * MLSTM_HEADS + h
        C0 = c_ref[st]
        n0 = n_ref[st, 0:1, :]
        m0 = m_ref[st, 0:1, 0:1]
        logD = jnp.where(incl, bc - br + li_r, -jnp.inf)
        m_prev = bc + m0
        m = jnp.maximum(m_prev, jnp.max(logD, axis=-1, keepdims=True))
        s = mm(qh, kh, NT) * jnp.exp(logD - m)
        inter = jnp.exp(m_prev - m)
        num = mm(s, vh) + inter * mm(qh, C0, NT)
        den = jnp.sum(s, axis=-1, keepdims=True) + inter * jnp.sum(qh * n0, axis=-1, keepdims=True)
        outs.append(num / jnp.maximum(jnp.abs(den), jnp.exp(-m)))
        bT = bc[edge:edge + 1, :]
        gsum = bT - bc + li_c
        m_new = jnp.maximum(bT + m0, jnp.max(gsum, axis=0, keepdims=True))
        wts = jnp.exp(gsum - m_new)
        cd = jnp.exp(bT + m0 - m_new)
        c_ref[st] = cd * C0 + mm(wts * vh, kh, TN)
        n_ref[st, 0:1, :] = cd * n0 + jnp.sum(wts * kh, axis=0, keepdims=True)
        m_ref[st] = jnp.broadcast_to(m_new, m_ref.shape[1:])
    return jnp.concatenate(outs, axis=1)


def _mlstm_kernel(qf_ref, kf_ref, vf_ref, gf_ref, qb_ref, kb_ref, vb_ref, gb_ref, gbias_ref,
                  of_ref, ob_ref, c_ref, n_ref, m_ref, *, prec):
    i = pl.program_id(1)

    @pl.when(i == 0)
    def _():
        c_ref[...] = jnp.zeros_like(c_ref)
        n_ref[...] = jnp.zeros_like(n_ref)
        m_ref[...] = jnp.zeros_like(m_ref)

    gbias = gbias_ref[...]
    of_ref[0] = _mlstm_dir(0, qf_ref[0], kf_ref[0], vf_ref[0], gf_ref[0], gbias, c_ref, n_ref, m_ref,
                           False, prec)
    ob_ref[0] = _mlstm_dir(1, qb_ref[0], kb_ref[0], vb_ref[0], gb_ref[0], gbias, c_ref, n_ref, m_ref,
                           True, prec)


def _mlstm(proj, gbias, n_lat, n_ctx, prec=2):
    b, ltot, _ = proj.shape
    T = T_MLSTM
    fwd = functools.partial(_fwd_blk, n_lat=n_lat, n_ctx=n_ctx)
    bwd = functools.partial(_bwd_blk, n_lat=n_lat, n_ctx=n_ctx)
    spec = lambda width, col, order: pl.BlockSpec((1, T, width), lambda bi, i: (bi, order(i), col))
    out = jax.ShapeDtypeStruct((b, ltot, MLSTM_W), F32)
    nst = 2 * MLSTM_HEADS
    return pl.pallas_call(
        functools.partial(_mlstm_kernel, prec=prec),
        grid=(b, n_lat + n_ctx),
        in_specs=[
            spec(MLSTM_W, EV_Q, fwd), spec(MLSTM_W, EV_K, fwd), spec(MLSTM_W, EV_V, fwd),
            spec(128, EV_GATES, fwd),
            spec(MLSTM_W, EV_Q, bwd), spec(MLSTM_W, EV_K, bwd), spec(MLSTM_W, EV_V, bwd),
            spec(128, EV_GATES, bwd),
            pl.BlockSpec((1, 128), lambda bi, i: (0, 0)),
        ],
        out_specs=[
            pl.BlockSpec((1, T, MLSTM_W), lambda bi, i: (bi, fwd(i), 0)),
            pl.BlockSpec((1, T, MLSTM_W), lambda bi, i: (bi, bwd(i), 0)),
        ],
        out_shape=[out, out],
        scratch_shapes=[
            pltpu.VMEM((nst, MLSTM_HD, MLSTM_HD), F32),
            pltpu.VMEM((nst, 8, MLSTM_HD), F32),
            pltpu.VMEM((nst, 8, 128), F32),
        ],
        compiler_params=pltpu.CompilerParams(
            dimension_semantics=("arbitrary", "arbitrary"), vmem_limit_bytes=VMEM_LIMIT),
        name="mlstm_chunked",
    )(proj, proj, proj, proj, proj, proj, proj, proj, gbias)


def _hgrn_dir(d, q, fpre, vi, lb, s_ref, reverse, p):
    T = q.shape[0]
    mm = functools.partial(_mm, pa=p, pb=p)
    incl, _ = _masks(T, reverse)
    edge = 0 if reverse else T - 1
    f = lb + (1.0 - lb) * _sigmoid(fpre)
    kk = 1.0 - f
    qs = _silu(q)
    bc = _mm(_ones_mask(incl), jnp.log(f), NN, 1, 3)
    bmid = bc[T // 2:T // 2 + 1, :]
    bT = bc[edge:edge + 1, :]
    qt = qs * jnp.exp(bc - bmid)
    kt = kk * jnp.exp(bmid - bc)
    qd = qs * jnp.exp(bc)
    kd = kk * jnp.exp(bT - bc)
    eT = jnp.exp(bT)
    outs = []
    for h in range(HGRN_HEADS):
        sl = slice(h * HGRN_FD, (h + 1) * HGRN_FD)
        st = d * HGRN_HEADS + h
        S = s_ref[st]
        A = jnp.where(incl, mm(qt[:, sl], kt[:, sl], NT), 0.0)
        outs.append(mm(A, vi[:, sl]) + mm(qd[:, sl], S, NT))
        s_ref[st] = S * eT[:, sl] + mm(vi[:, sl], kd[:, sl], TN)
    return jnp.concatenate(outs, axis=1)


def _hgrn_kernel(qf_ref, ff_ref, if_ref, qb_ref, fb_ref, ib_ref, lb_ref, of_ref, ob_ref, s_ref,
                 *, layer, prec):
    i = pl.program_id(1)

    @pl.when(i == 0)
    def _():
        s_ref[...] = jnp.zeros_like(s_ref)

    raw = lb_ref[...]
    e = jnp.exp(raw - jnp.max(raw, axis=0, keepdims=True))
    sm = e / jnp.sum(e, axis=0, keepdims=True)
    lb = jnp.sum(sm[:layer + 1], axis=0, keepdims=True) - sm[0:1]
    of_ref[0] = _hgrn_dir(0, qf_ref[0], ff_ref[0], if_ref[0], lb, s_ref, False, prec)
    ob_ref[0] = _hgrn_dir(1, qb_ref[0], fb_ref[0], ib_ref[0], lb, s_ref, True, prec)


def _hgrn(proj, hgrn_lb, layer, n_lat, n_ctx, prec=2):
    b, ltot, _ = proj.shape
    T = T_HGRN
    fwd = functools.partial(_fwd_blk, n_lat=n_lat, n_ctx=n_ctx)
    bwd = functools.partial(_bwd_blk, n_lat=n_lat, n_ctx=n_ctx)
    spec = lambda col, order: pl.BlockSpec((1, T, HGRN_W), lambda bi, i: (bi, order(i), col))
    out = jax.ShapeDtypeStruct((b, ltot, HGRN_W), F32)
    return pl.pallas_call(
        functools.partial(_hgrn_kernel, layer=layer, prec=prec),
        grid=(b, n_lat + n_ctx),
        in_specs=[
            spec(OD_Q, fwd), spec(OD_F, fwd), spec(OD_I, fwd),
            spec(OD_Q, bwd), spec(OD_F + 1, bwd), spec(OD_I, bwd),
            pl.BlockSpec(hgrn_lb.shape, lambda bi, i: (0, 0)),
        ],
        out_specs=[
            pl.BlockSpec((1, T, HGRN_W), lambda bi, i: (bi, fwd(i), 0)),
            pl.BlockSpec((1, T, HGRN_W), lambda bi, i: (bi, bwd(i), 0)),
        ],
        out_shape=[out, out],
        scratch_shapes=[pltpu.VMEM((2 * HGRN_HEADS, HGRN_FD, HGRN_FD), F32)],
        compiler_params=pltpu.CompilerParams(
            dimension_semantics=("arbitrary", "arbitrary"), vmem_limit_bytes=VMEM_LIMIT),
        name="hgrn2_chunked",
    )(proj, proj, proj, proj, proj, proj, hgrn_lb)


def _lru_conv(x, halo_prev, halo_next, has_prev, has_next, cw, cb):
    T = x.shape[0]
    row = lax.broadcasted_iota(jnp.int32, x.shape, 0)
    hp = jnp.where(has_prev, halo_prev, 0.0)
    hn = jnp.where(has_next, halo_next, 0.0)
    x_m1 = jnp.where(row == 0, hp[7:8, :], pltpu.roll(x, 1, 0))
    x_m2 = jnp.where(row == 0, hp[6:7, :], jnp.where(row == 1, hp[7:8, :], pltpu.roll(x, 2, 0)))
    x_p1 = jnp.where(row == T - 1, hn[0:1, :], pltpu.roll(x, T - 1, 0))
    return x_m2 * cw[0:1, :] + x_m1 * cw[1:2, :] + x * cw[2:3, :] + x_p1 * cw[3:4, :] + cb


def _lru_dir(d, xc, wg, bias, sp_lam, h_ref, reverse, p):
    T = xc.shape[0]
    gates = _sigmoid(_mm(xc, wg, NN, p, p) + bias)
    r = gates[:, :LRU_W]
    ig = gates[:, LRU_W:]
    log_a = -LRU_C * r * sp_lam
    a = jnp.exp(log_a)
    th = jnp.tanh(log_a)
    u = jnp.sqrt(-2.0 * th / (1.0 - th)) * (ig * xc)
    row = lax.broadcasted_iota(jnp.int32, xc.shape, 0)
    k = 1
    while k < T:
        if reverse:
            ok = row < T - k
            a_n = jnp.where(ok, pltpu.roll(a, T - k, 0), 1.0)
            u_n = jnp.where(ok, pltpu.roll(u, T - k, 0), 0.0)
        else:
            ok = row >= k
            a_n = jnp.where(ok, pltpu.roll(a, k, 0), 1.0)
            u_n = jnp.where(ok, pltpu.roll(u, k, 0), 0.0)
        u = a * u_n + u
        a = a * a_n
        k *= 2
    hs = a * h_ref[d, 0:1, :] + u
    edge = 0 if reverse else T - 1
    h_ref[d, 0:1, :] = hs[edge:edge + 1, :]
    return hs


def _lru_kernel(xf_ref, pf_ref, nf_ref, xb_ref, pb_ref, nb_ref, cw_ref, cb_ref, wg_ref, bias_ref,
                lam_ref, of_ref, ob_ref, h_ref, *, n_lat, n_ctx, prec):
    i = pl.program_id(1)

    @pl.when(i == 0)
    def _():
        h_ref[...] = jnp.zeros_like(h_ref)

    cw = cw_ref[...]
    cb = cb_ref[...]
    for d, (x_ref, p_ref, n_ref, o_ref) in enumerate(
            ((xf_ref, pf_ref, nf_ref, of_ref), (xb_ref, pb_ref, nb_ref, ob_ref))):
        blk = _bwd_blk(i, n_lat, n_ctx) if d else _fwd_blk(i, n_lat, n_ctx)
        has_prev = (blk != 0) & (blk != n_lat)
        has_next = (blk != n_lat - 1) & (blk != n_lat + n_ctx - 1)
        xc = _lru_conv(x_ref[0], p_ref[0], n_ref[0], has_prev, has_next, cw, cb)
        sp_lam = _softplus(-lam_ref[d])
        o_ref[0] = _lru_dir(d, xc, wg_ref[d], bias_ref[d], sp_lam, h_ref, bool(d), prec)


def _lru(proj, conv_w, conv_b, wg, bias, lam, n_lat, n_ctx, prec=2):
    b, ltot, _ = proj.shape
    T = T_LRU
    r8 = T // 8
    n8 = ltot // 8
    fwd = functools.partial(_fwd_blk, n_lat=n_lat, n_ctx=n_ctx)
    bwd = functools.partial(_bwd_blk, n_lat=n_lat, n_ctx=n_ctx)
    col8 = OD_XB
    cur = lambda order: pl.BlockSpec((1, T, LRU_W), lambda bi, i: (bi, order(i), OD_XB))
    prev = lambda order: pl.BlockSpec(
        (1, 8, LRU_W), lambda bi, i: (bi, jnp.maximum(order(i) * r8 - 1, 0), col8))
    nxt = lambda order: pl.BlockSpec(
        (1, 8, LRU_W), lambda bi, i: (bi, jnp.minimum((order(i) + 1) * r8, n8 - 1), col8))
    full = lambda shape: pl.BlockSpec(shape, lambda bi, i: (0,) * len(shape))
    out = jax.ShapeDtypeStruct((b, ltot, LRU_W), F32)
    return pl.pallas_call(
        functools.partial(_lru_kernel, n_lat=n_lat, n_ctx=n_ctx, prec=prec),
        grid=(b, n_lat + n_ctx),
        in_specs=[
            cur(fwd), prev(fwd), nxt(fwd), cur(bwd), prev(bwd), nxt(bwd),
            full((4, LRU_W)), full((1, LRU_W)), full((2, LRU_W, 2 * LRU_W)), full((2, 1, 2 * LRU_W)),
            full((2, 1, LRU_W)),
        ],
        out_specs=[
            pl.BlockSpec((1, T, LRU_W), lambda bi, i: (bi, fwd(i), 0)),
            pl.BlockSpec((1, T, LRU_W), lambda bi, i: (bi, bwd(i), 0)),
        ],
        out_shape=[out, out],
        scratch_shapes=[pltpu.VMEM((2, 8, LRU_W), F32)],
        compiler_params=pltpu.CompilerParams(
            dimension_semantics=("arbitrary", "arbitrary"), vmem_limit_bytes=VMEM_LIMIT),
        name="rglru_scan",
    )(proj, proj, proj, proj, proj, proj, conv_w, conv_b, wg, bias, lam)


def _head_rms(x, heads, width):
    outs = []
    for h in range(heads):
        xh = x[:, h * width:(h + 1) * width]
        outs.append(xh * lax.rsqrt(jnp.mean(xh * xh, axis=-1, keepdims=True) + EPS))
    return jnp.concatenate(outs, axis=1)


def _out_even_kernel(h_ref, rf_ref, rb_ref, gd_ref, mf_ref, mb_ref, op_ref, g2_ref, ng_ref, w_ref,
                     mod_ref, o_ref):
    m = mod_ref[0]
    g = _mm(_sigmoid(gd_ref[0]), g2_ref[...], NN, 2, 2)
    y1 = (rf_ref[0] + rb_ref[0]) * g
    y2 = _head_rms(mf_ref[0] + mb_ref[0], MLSTM_HEADS, MLSTM_HD) * ng_ref[...] * _sigmoid(op_ref[0])
    y = jnp.concatenate([y1, y2], axis=1).astype(BF16)
    o_ref[0] = h_ref[0] + m[2:3] * jnp.dot(y, w_ref[...], preferred_element_type=F32)


def _out_even(h_all, rf, rb, proj, mf, mb, g2, ng, w_out, mod, n_lat_tiles):
    b, ltot, d = h_all.shape
    tok = lambda width, col=0: pl.BlockSpec((1, TM, width), lambda i, t: (i, t, col))
    full = lambda shape: pl.BlockSpec(shape, lambda i, t: (0,) * len(shape))
    return pl.pallas_call(
        _out_even_kernel,
        grid=(b, ltot // TM),
        in_specs=[
            tok(d), tok(RWKV_W), tok(RWKV_W), tok(128, EV_GD), tok(MLSTM_W), tok(MLSTM_W),
            tok(MLSTM_W, EV_O), full((GATE_LORA, RWKV_W)), full((1, MLSTM_W)), full((d, d)),
            pl.BlockSpec((1, 6, d), lambda i, t: (jnp.where(t >= n_lat_tiles, 4, i), 0, 0)),
        ],
        out_specs=tok(d),
        out_shape=jax.ShapeDtypeStruct(h_all.shape, F32),
        input_output_aliases={0: 0},
        compiler_params=pltpu.CompilerParams(vmem_limit_bytes=VMEM_LIMIT),
        name="out_proj_even",
    )(h_all, rf, rb, proj, mf, mb, proj, g2, ng, w_out, mod)


def _gelu_tanh(x):
    return 0.5 * x * (1.0 + jnp.tanh(0.7978845608028654 * (x + 0.044715 * x * x * x)))


def _out_odd_kernel(h_ref, hf_ref, hb_ref, g_ref, lf_ref, lb_ref, gb_ref, ng_ref, w_ref, mod_ref, o_ref):
    m = mod_ref[0]
    y1 = _head_rms(hf_ref[0] + hb_ref[0], HGRN_HEADS, HGRN_FD) * ng_ref[...] * _silu(g_ref[0])
    y2 = (lf_ref[0] + lb_ref[0]) * _gelu_tanh(gb_ref[0])
    y = jnp.concatenate([y1, y2], axis=1).astype(BF16)
    o_ref[0] = h_ref[0] + m[2:3] * jnp.dot(y, w_ref[...], preferred_element_type=F32)


def _out_odd(h_all, hf, hb, proj, lf, lb, ng, w_out, mod, rows):
    b, ltot, d = h_all.shape
    h_strided = h_all.reshape(b, ltot // GRID_W, GRID_W * d)
    tok = lambda width, col=0: pl.BlockSpec((1, rows, width), lambda i, t: (i, t, col))
    full = lambda shape: pl.BlockSpec(shape, lambda i, t: (0,) * len(shape))
    hspec = pl.BlockSpec((1, rows, d), lambda i, t: (i, 0, t))
    out = pl.pallas_call(
        _out_odd_kernel,
        grid=(b, GRID_W),
        in_specs=[
            hspec, tok(HGRN_W), tok(HGRN_W), tok(HGRN_W, OD_G), tok(LRU_W), tok(LRU_W), tok(LRU_W, OD_GB),
            full((1, HGRN_W)), full((d, d)),
            pl.BlockSpec((1, 6, d), lambda i, t: (i, 0, 0)),
        ],
        out_specs=hspec,
        out_shape=jax.ShapeDtypeStruct(h_strided.shape, F32),
        input_output_aliases={0: 0},
        compiler_params=pltpu.CompilerParams(vmem_limit_bytes=VMEM_LIMIT),
        name="out_proj_odd",
    )(h_strided, hf, hb, proj, lf, lb, proj, ng, w_out, mod)
    return out.reshape(b, ltot, d)


def _ffn_kernel(h_ref, g_ref, mod_ref, w1_ref, w3_ref, w2_ref, fg_ref, o_ref, *, final):
    m = mod_ref[0]
    h = h_ref[0]
    v = _norm_mod(h, g_ref[...], m[3:4], m[4:5]).astype(BF16)
    a = jnp.dot(v, w1_ref[...], preferred_element_type=F32)
    c = jnp.dot(v, w3_ref[...], preferred_element_type=F32)
    hid = (_silu(a) * c).astype(BF16)
    out = h + m[5:6] * jnp.dot(hid, w2_ref[...], preferred_element_type=F32)
    if final:
        ms = jnp.mean(out * out, axis=-1, keepdims=True)
        out = out * lax.rsqrt(ms + EPS) * fg_ref[...]
    o_ref[0] = out


def _ffn(h_all, g, mod, w1, w3, w2, final_g, n_lat_tiles, final):
    b, ltot, d = h_all.shape
    ff = w1.shape[1]
    nt = n_lat_tiles if final else ltot // TM
    tok = pl.BlockSpec((1, TM, d), lambda i, t: (i, t, 0))
    full = lambda shape: pl.BlockSpec(shape, lambda i, t: (0,) * len(shape))
    return pl.pallas_call(
        functools.partial(_ffn_kernel, final=final),
        grid=(b, nt),
        in_specs=[
            tok, full((1, d)),
            pl.BlockSpec((1, 6, d), lambda i, t: (jnp.where(t >= n_lat_tiles, 4, i), 0, 0)),
            full((d, ff)), full((d, ff)), full((ff, d)), full((1, d)),
        ],
        out_specs=tok,
        out_shape=jax.ShapeDtypeStruct((b, nt * TM, d), F32),
        compiler_params=pltpu.CompilerParams(vmem_limit_bytes=VMEM_LIMIT),
        name="ffn_final" if final else "ffn",
    )(h_all, g, mod, w1, w3, w2, final_g)


def _pack_even_in_w(w):
    r, k, v, wd, ad, gd, q, mk, mv, mo, ip, fp = jnp.split(
        w, [512, 1024, 1536, 1664, 1792, 1920, 2432, 2944, 3456, 3968, 3976], axis=1)
    lora = jnp.concatenate([wd[:, :LORA], ad[:, :LORA], wd[:, LORA:], ad[:, LORA:]], axis=1)
    gates = jnp.concatenate([ip, fp, jnp.zeros((w.shape[0], 128 - 16), w.dtype)], axis=1)
    return jnp.concatenate([r, k, v, q, mk, mv, mo, lora, gd, gates], axis=1).astype(BF16)


def _pack_rwkv_params(mu, w0, w2, a0, a2, kk, ka, rk, lnw, lnb):
    mu_x = mu[:, None, :3 * RWKV_W]
    mu_lo = mu[:, None, 3 * RWKV_W:]
    p512 = jnp.stack([w0, a0, kk, ka, lnw, lnb, rk.reshape(2, RWKV_W), jnp.zeros_like(w0)], axis=1)
    z = jnp.zeros_like(w2)
    wcomb = jnp.concatenate([jnp.concatenate([w2, z], axis=2), jnp.concatenate([z, a2], axis=2)], axis=1)
    return mu_x, mu_lo, p512, wcomb


def _pack_lru_params(wa, ba, wi, bi):
    def dense(w):
        eye = jnp.eye(LRU_BLOCKS, dtype=w.dtype)
        return jnp.einsum('dhij,hg->dhigj', w, eye).reshape(2, LRU_W, LRU_W)
    wg = jnp.concatenate([dense(wa), dense(wi)], axis=2)
    bias = jnp.concatenate([ba, bi], axis=1)[:, None, :]
    return wg, bias


def kernel(x, c, ctx, c_ctx, norm1_g, norm2_g, mod_w, mod_b, ffn_w1, ffn_w3, ffn_w2, final_g, ev_in_w, ev_out_w, rwkv_mu, rwkv_w0, rwkv_w2, rwkv_a0, rwkv_a2, rwkv_kk, rwkv_ka, rwkv_rk, rwkv_lnw, rwkv_lnb, rwkv_g2, mlstm_bi, mlstm_bf, mlstm_ng, od_in_w, od_out_w, hgrn_lb, hgrn_ng, lru_conv_w, lru_conv_b, lru_wa, lru_ba, lru_wi, lru_bi, lru_lam):
    b, seq, d = x.shape
    n_ctx_tok = ctx.shape[1]
    rows = seq // GRID_W
    n_lat_tiles = seq // TM
    depth = mod_w.shape[0]

    c8 = jnp.concatenate([c, c_ctx[None, :], jnp.zeros((8 - b - 1, d), F32)], axis=0)
    mod = _modulation(c8, mod_w, mod_b).reshape(depth, 8, 6, d)
    h_all = jnp.concatenate([x, ctx], axis=1)

    for l in range(depth):
        j = l // 2
        last = l == depth - 1
        g1 = norm1_g[l][None, :]
        if l % 2 == 0:
            proj = _in_proj_even(h_all, g1, mod[l], _pack_even_in_w(ev_in_w[j]), n_lat_tiles)
            mu_x, mu_lo, p512, wcomb = _pack_rwkv_params(
                rwkv_mu[j], rwkv_w0[j], rwkv_w2[j], rwkv_a0[j], rwkv_a2[j], rwkv_kk[j], rwkv_ka[j],
                rwkv_rk[j], rwkv_lnw[j], rwkv_lnb[j])
            rf, rb = _rwkv(proj, mu_x, mu_lo, p512, wcomb, seq // T_RWKV, n_ctx_tok // T_RWKV)
            gbias = jnp.concatenate([mlstm_bi[j].reshape(-1), mlstm_bf[j].reshape(-1),
                                     jnp.zeros((128 - 4 * MLSTM_HEADS,), F32)])[None, :]
            mf, mb = _mlstm(proj, gbias, seq // T_MLSTM, n_ctx_tok // T_MLSTM)
            h_all = _out_even(h_all, rf, rb, proj, mf, mb, rwkv_g2[j], mlstm_ng[j][None, :],
                              ev_out_w[j].astype(BF16), mod[l], n_lat_tiles)
        else:
            proj = _in_proj_odd(h_all, g1, mod[l], od_in_w[j].astype(BF16), rows)
            hf, hb = _hgrn(proj, hgrn_lb, l, seq // T_HGRN, n_ctx_tok // T_HGRN)
            wg, bias = _pack_lru_params(lru_wa[j], lru_ba[j], lru_wi[j], lru_bi[j])
            lf, lb = _lru(proj, lru_conv_w[j], lru_conv_b[j][None, :], wg, bias, lru_lam[j][:, None, :],
                          seq // T_LRU, n_ctx_tok // T_LRU)
            h_all = _out_odd(h_all, hf, hb, proj, lf, lb, hgrn_ng[j][None, :], od_out_w[j].astype(BF16),
                             mod[l], rows)
        h_all = _ffn(h_all, norm2_g[l][None, :], mod[l], ffn_w1[l].astype(BF16), ffn_w3[l].astype(BF16),
                     ffn_w2[l].astype(BF16), final_g[None, :], n_lat_tiles, last)
    return h_all
```

```python
import functools

import jax
import jax.numpy as jnp
from jax import lax
from jax.experimental import pallas as pl
from jax.experimental.pallas import tpu as pltpu

F32 = jnp.float32
BF16 = jnp.bfloat16

D_MODEL = 1024
GRID_W = 64
D_FF = 2816
EPS = 1e-6

RWKV_HEADS = 8
RWKV_HD = 64
RWKV_W = RWKV_HEADS * RWKV_HD
LORA = 64
GATE_LORA = 128
RWKV_GN_EPS = 64e-5
MLSTM_HEADS = 4
MLSTM_HD = 128
MLSTM_W = MLSTM_HEADS * MLSTM_HD
HGRN_HEADS = 4
HGRN_FD = 128
HGRN_W = HGRN_HEADS * HGRN_FD
LRU_W = 512
LRU_BLOCKS = 8
LRU_BD = LRU_W // LRU_BLOCKS
LRU_C = 8.0

T_RWKV = 64
T_MLSTM = 128
T_HGRN = 64
T_LRU = 256
TM = 256
TM_ODD = 8 * GRID_W

EV_N = 4096
EV_RKV, EV_Q, EV_K, EV_V, EV_O = 0, 3, 4, 5, 6
EV_LORA, EV_GD, EV_GATES = 28, 30, 31
OD_N = 3584
OD_Q, OD_F, OD_I, OD_G, OD_XB, OD_GB = 0, 1, 3, 4, 5, 6

VMEM_LIMIT = 56 * 1024 * 1024

NT = ((1,), (1,))
NN = ((1,), (0,))
TN = ((0,), (0,))


def _split(x, n):
    parts = []
    r = x
    for i in range(n):
        p = r.astype(BF16)
        parts.append(p)
        if i + 1 < n:
            r = r - p.astype(F32)
    return parts


def _mm(a, b, dims=NN, pa=1, pb=1):
    dn = (dims, ((), ()))
    ap = _split(a, pa) if a.dtype != BF16 else [a]
    bp = _split(b, pb) if b.dtype != BF16 else [b]
    n = max(len(ap), len(bp))
    acc = None
    for i, x in enumerate(ap):
        for j, y in enumerate(bp):
            if i + j < n:
                t = lax.dot_general(x, y, dn, preferred_element_type=F32)
                acc = t if acc is None else acc + t
    return acc


def _sigmoid(x):
    return 1.0 / (1.0 + jnp.exp(-x))


def _silu(x):
    return x * _sigmoid(x)


def _log_sigmoid(x):
    return jnp.minimum(x, 0.0) - jnp.log1p(jnp.exp(-jnp.abs(x)))


def _softplus(x):
    return jnp.maximum(x, 0.0) + jnp.log1p(jnp.exp(-jnp.abs(x)))


def _masks(T, reverse):
    ti = lax.broadcasted_iota(jnp.int32, (T, T), 0)
    si = lax.broadcasted_iota(jnp.int32, (T, T), 1)
    if reverse:
        return si >= ti, si > ti
    return si <= ti, si < ti


def _ones_mask(incl):
    return jnp.where(incl, 1.0, 0.0).astype(BF16)


def _shift_prev(x, carry_row, reverse):
    T = x.shape[0]
    row = lax.broadcasted_iota(jnp.int32, x.shape, 0)
    if reverse:
        return jnp.where(row == T - 1, carry_row, pltpu.roll(x, T - 1, 0))
    return jnp.where(row == 0, carry_row, pltpu.roll(x, 1, 0))


def _fwd_blk(i, n_lat, n_ctx):
    return jnp.where(i < n_ctx, n_lat + i, i - n_ctx)


def _bwd_blk(i, n_lat, n_ctx):
    return n_lat + n_ctx - 1 - i


def _mod_kernel(c_ref, w_ref, b_ref, o_ref):
    c = c_ref[...]
    o_ref[0] = _mm(_silu(c), w_ref[0], NN, 3, 3) + b_ref[0]


def _modulation(c8, mod_w, mod_b):
    depth, d, n = mod_w.shape
    return pl.pallas_call(
        _mod_kernel,
        grid=(depth, n // d),
        in_specs=[
            pl.BlockSpec((8, d), lambda l, j: (0, 0)),
            pl.BlockSpec((1, d, d), lambda l, j: (l, 0, j)),
            pl.BlockSpec((1, 1, d), lambda l, j: (l, 0, j)),
        ],
        out_specs=pl.BlockSpec((1, 8, d), lambda l, j: (l, 0, j)),
        out_shape=jax.ShapeDtypeStruct((depth, 8, n), F32),
        compiler_params=pltpu.CompilerParams(vmem_limit_bytes=VMEM_LIMIT),
        name="adaln_modulation",
    )(c8, mod_w, mod_b.reshape(depth, 1, n))


def _norm_mod(x, g, shift, scale):
    ms = jnp.mean(x * x, axis=-1, keepdims=True)
    return (x * lax.rsqrt(ms + EPS) * g) * (1.0 + scale) + shift


def _in_even_kernel(x_ref, g_ref, mod_ref, w_ref, o_ref):
    m = mod_ref[0]
    u = _norm_mod(x_ref[0], g_ref[...], m[0:1], m[1:2])
    o_ref[0] = jnp.dot(u.astype(BF16), w_ref[...], preferred_element_type=F32)


def _in_proj_even(h_all, g, mod, w, n_lat_tiles):
    b, ltot, d = h_all.shape
    n = w.shape[1]
    nt = ltot // TM
    return pl.pallas_call(
        _in_even_kernel,
        grid=(b, nt),
        in_specs=[
            pl.BlockSpec((1, TM, d), lambda i, t: (i, t, 0)),
            pl.BlockSpec((1, d), lambda i, t: (0, 0)),
            pl.BlockSpec((1, 6, d), lambda i, t: (jnp.where(t >= n_lat_tiles, 4, i), 0, 0)),
            pl.BlockSpec((d, n), lambda i, t: (0, 0)),
        ],
        out_specs=pl.BlockSpec((1, TM, n), lambda i, t: (i, t, 0)),
        out_shape=jax.ShapeDtypeStruct((b, ltot, n), F32),
        compiler_params=pltpu.CompilerParams(vmem_limit_bytes=VMEM_LIMIT),
        name="in_proj_even",
    )(h_all, g, mod, w)


def _in_odd_kernel(x_ref, g_ref, mod_ref, w_ref, *rest, scatter):
    o_ref = rest[-1]
    m = mod_ref[0]
    u = _norm_mod(x_ref[0], g_ref[...], m[0:1], m[1:2])
    y = jnp.dot(u.astype(BF16), w_ref[...], preferred_element_type=F32)
    if scatter:
        for r in range(o_ref.shape[2]):
            o_ref[0, :, r, :] = y[r * GRID_W:(r + 1) * GRID_W, :]
    else:
        for j in range(o_ref.shape[1]):
            o_ref[0, j] = y[j * o_ref.shape[2]:(j + 1) * o_ref.shape[2], :]


def _in_proj_odd(h_all, g, mod, w, rows):
    b, ltot, d = h_all.shape
    n = w.shape[1]
    lat = rows * GRID_W
    n_ctx_slots = (ltot - lat) // rows
    tr = TM_ODD // GRID_W
    out_shape = jax.ShapeDtypeStruct((b, GRID_W + n_ctx_slots, rows, n), F32)
    common = [
        pl.BlockSpec((1, d), lambda i, t: (0, 0)),
        None,
        pl.BlockSpec((d, n), lambda i, t: (0, 0)),
    ]
    lat_specs = list(common)
    lat_specs[1] = pl.BlockSpec((1, 6, d), lambda i, t: (i, 0, 0))
    proj = pl.pallas_call(
        functools.partial(_in_odd_kernel, scatter=True),
        grid=(b, lat // TM_ODD),
        in_specs=[pl.BlockSpec((1, TM_ODD, d), lambda i, t: (i, t, 0))] + lat_specs,
        out_specs=pl.BlockSpec((1, GRID_W, tr, n), lambda i, t: (i, 0, t, 0)),
        out_shape=out_shape,
        compiler_params=pltpu.CompilerParams(vmem_limit_bytes=VMEM_LIMIT),
        name="in_proj_odd",
    )(h_all, g, mod, w)
    ctx_tok = ltot - lat
    ctx_specs = list(common)
    ctx_specs[1] = pl.BlockSpec((1, 6, d), lambda i, t: (4, 0, 0))
    return pl.pallas_call(
        functools.partial(_in_odd_kernel, scatter=False),
        grid=(b, 1),
        in_specs=[pl.BlockSpec((1, ctx_tok, d), lambda i, t: (i, lat // ctx_tok, 0))] + ctx_specs
        + [pl.BlockSpec(memory_space=pl.ANY)],
        out_specs=pl.BlockSpec((1, n_ctx_slots, rows, n), lambda i, t: (i, GRID_W // n_ctx_slots, 0, 0)),
        out_shape=out_shape,
        input_output_aliases={4: 0},
        compiler_params=pltpu.CompilerParams(vmem_limit_bytes=VMEM_LIMIT),
        name="in_proj_odd_ctx",
    )(h_all, g, mod, w, proj).reshape(b, ltot, n)


def _rwkv_prep(x, lo, carry_x, carry_lo, mu_x, mu_lo, p512, wcomb, ones_blk, reverse):
    T = x.shape[0]
    incl, _ = _masks(T, reverse)
    edge = 0 if reverse else T - 1
    xs = x + mu_x * (_shift_prev(x, carry_x, reverse) - x)
    los = lo + mu_lo * (_shift_prev(lo, carry_lo, reverse) - lo)
    r = xs[:, :RWKV_W]
    k = xs[:, RWKV_W:2 * RWKV_W]
    v = xs[:, 2 * RWKV_W:]
    w0, a0, k_k, k_a = (p512[i:i + 1, :] for i in range(4))
    lane = lax.broadcasted_iota(jnp.int32, los.shape, 1)
    z = jnp.where(lane < LORA, jnp.tanh(los), los)
    wa = _mm(z, wcomb, NN, 2, 2)
    w = -_softplus(-(w0 + wa[:, :RWKV_W])) - 0.5
    a = _sigmoid(a0 + wa[:, RWKV_W:])
    logdec = -jnp.exp(w)
    kk = k * k_k
    ss = _mm(kk * kk, ones_blk, NN, 2, 1)
    kk = kk / jnp.maximum(jnp.sqrt(ss), 1e-12)
    k2 = k * (1.0 + (a - 1.0) * k_a)
    cl = _mm(_ones_mask(incl), logdec, NN, 1, 3)
    e_pos = jnp.exp(cl)
    e_neg = jnp.exp(-cl)
    return dict(
        A=kk * jnp.exp(cl - logdec), B=kk * a * e_neg, K=k2 * e_neg, R=r * e_pos, V=v,
        gT=e_pos[edge:edge + 1, :], r=r, k2=k2,
        carry_x=x[edge:edge + 1, :], carry_lo=lo[edge:edge + 1, :])


def _rwkv_post(o, u, p512, ones_blk):
    ln_w, ln_b, r_k = (p512[i:i + 1, :] for i in range(4, 7))
    inv_n = 1.0 / RWKV_HD
    mean = _mm(o, ones_blk, NN, 2, 1) * inv_n
    oc = o - mean
    var = _mm(oc * oc, ones_blk, NN, 2, 1) * inv_n
    gn = oc * lax.rsqrt(var + RWKV_GN_EPS) * ln_w + ln_b
    bonus = _mm(u['r'] * u['k2'] * r_k, ones_blk, NN, 2, 1) * u['V']
    return gn + bonus


def _rwkv_chunks(units, states, T, p):
    mm = functools.partial(_mm, pa=p, pb=p)
    jobs = []
    for ui, (u, reverse) in enumerate(units):
        incl, strict = _masks(T, reverse)
        for h in range(RWKV_HEADS):
            sl = slice(h * RWKV_HD, (h + 1) * RWKV_HD)
            jobs.append(dict(
                ui=ui, incl=incl, strict=strict,
                AR=jnp.concatenate([u['A'][:, sl], u['R'][:, sl]], axis=0),
                B=u['B'][:, sl], K=u['K'][:, sl], V=u['V'][:, sl], gT=u['gT'][:, sl],
                S=states[ui][h]))
    for j in jobs:
        j['GB'] = mm(j['AR'], j['B'], NT)
    for j in jobs:
        j['GK'] = mm(j['AR'], j['K'], NT)
    for j in jobs:
        j['SS'] = mm(j['AR'], j['S'], NT)
    for j in jobs:
        both = jnp.concatenate([j['strict'], j['incl']], axis=0)
        j['LV'] = mm(jnp.where(both, j['GK'], 0.0), j['V'])
        j['N'] = jnp.where(j['strict'], j['GB'][:T], 0.0)
    ri = lax.broadcasted_iota(jnp.int32, (T, T), 0)
    ci = lax.broadcasted_iota(jnp.int32, (T, T), 1)
    rc = ri ^ ci
    eye = jnp.where(rc == 0, 1.0, 0.0)
    for j in jobs:
        j['D'] = eye - jnp.where(rc == 1, j['N'], 0.0)
    s = 2
    while s < T:
        level = (rc >= s) & (rc < 2 * s)
        for j in jobs:
            j['LD'] = mm(jnp.where(level, j['N'], 0.0), j['D'])
        for j in jobs:
            j['D'] = j['D'] - mm(j['D'], j['LD'])
        s *= 2
    for j in jobs:
        j['X'] = mm(j['D'], j['SS'][:T] + j['LV'][:T])
    for j in jobs:
        prb = jnp.where(j['incl'], j['GB'][T:], 0.0)
        j['O'] = j['SS'][T:] + j['LV'][T:] - mm(prb, j['X'])
    for j in jobs:
        dS = mm(jnp.concatenate([j['V'], -j['X']], axis=0), jnp.concatenate([j['K'], j['B']], axis=0), TN)
        j['S_new'] = (j['S'] + dS) * j['gT']
    outs, new_states = [], []
    for ui in range(len(units)):
        mine = [j for j in jobs if j['ui'] == ui]
        outs.append(jnp.concatenate([j['O'] for j in mine], axis=1))
        new_states.append([j['S_new'] for j in mine])
    return outs, new_states


def _rwkv_kernel(xf_ref, xb_ref, lf_ref, lb_ref, mux_ref, mulo_ref, p512_ref, wcomb_ref, ones_ref,
                 of_ref, ob_ref, s_ref, cx_ref, cl_ref, *, n_ctx, prec):
    i = pl.program_id(1)
    T = xf_ref.shape[1]

    @pl.when(i == 0)
    def _():
        s_ref[...] = jnp.zeros_like(s_ref)

    @pl.when((i == 0) | (i == n_ctx))
    def _():
        cx_ref[...] = jnp.zeros_like(cx_ref)
        cl_ref[...] = jnp.zeros_like(cl_ref)

    ones_blk = ones_ref[...]
    refs = ((xf_ref, lf_ref, of_ref), (xb_ref, lb_ref, ob_ref))
    units, states, where = [], [], []
    for bi in range(xf_ref.shape[0]):
        for d, (x_ref, lo_ref, _) in enumerate(refs):
            u = _rwkv_prep(x_ref[bi], lo_ref[bi], cx_ref[bi, d, 0:1, :], cl_ref[bi, d, 0:1, :], mux_ref[d],
                           mulo_ref[d], p512_ref[d], wcomb_ref[d], ones_blk, bool(d))
            cx_ref[bi, d, 0:1, :] = u['carry_x']
            cl_ref[bi, d, 0:1, :] = u['carry_lo']
            units.append((u, bool(d)))
            states.append([s_ref[bi, d, h] for h in range(RWKV_HEADS)])
            where.append((bi, d))
    outs, new_states = _rwkv_chunks(units, states, T, prec)
    for (bi, d), (u, _), o, ns in zip(where, units, outs, new_states):
        for h in range(RWKV_HEADS):
            s_ref[bi, d, h] = ns[h]
        refs[d][2][bi] = _rwkv_post(o, u, p512_ref[d], ones_blk)


def _rwkv(proj, mu_x, mu_lo, p512, wcomb, n_lat, n_ctx, prec=1, bb=1):
    b, ltot, _ = proj.shape
    T = T_RWKV
    ones_blk = jnp.kron(jnp.eye(RWKV_HEADS, dtype=F32), jnp.ones((RWKV_HD, RWKV_HD), F32)).astype(BF16)
    fwd = functools.partial(_fwd_blk, n_lat=n_lat, n_ctx=n_ctx)
    bwd = functools.partial(_bwd_blk, n_lat=n_lat, n_ctx=n_ctx)
    full = lambda shape: pl.BlockSpec(shape, lambda bi, i: (0,) * len(shape))
    out = jax.ShapeDtypeStruct((b, ltot, RWKV_W), F32)
    return pl.pallas_call(
        functools.partial(_rwkv_kernel, n_ctx=n_ctx, prec=prec),
        grid=(b // bb, n_lat + n_ctx),
        in_specs=[
            pl.BlockSpec((bb, T, 3 * RWKV_W), lambda bi, i: (bi, fwd(i), EV_RKV)),
            pl.BlockSpec((bb, T, 3 * RWKV_W), lambda bi, i: (bi, bwd(i), EV_RKV)),
            pl.BlockSpec((bb, T, 128), lambda bi, i: (bi, fwd(i), EV_LORA)),
            pl.BlockSpec((bb, T, 128), lambda bi, i: (bi, bwd(i), EV_LORA + 1)),
            full((2, 1, 3 * RWKV_W)), full((2, 1, 128)), full((2, 8, RWKV_W)),
            full((2, 128, 2 * RWKV_W)), full((RWKV_W, RWKV_W)),
        ],
        out_specs=[
            pl.BlockSpec((bb, T, RWKV_W), lambda bi, i: (bi, fwd(i), 0)),
            pl.BlockSpec((bb, T, RWKV_W), lambda bi, i: (bi, bwd(i), 0)),
        ],
        out_shape=[out, out],
        scratch_shapes=[
            pltpu.VMEM((bb, 2, RWKV_HEADS, RWKV_HD, RWKV_HD), F32),
            pltpu.VMEM((bb, 2, 8, 3 * RWKV_W), F32),
            pltpu.VMEM((bb, 2, 8, 128), F32),
        ],
        compiler_params=pltpu.CompilerParams(
            dimension_semantics=("arbitrary", "arbitrary"), vmem_limit_bytes=VMEM_LIMIT),
        name="rwkv7_chunked",
    )(proj, proj, proj, proj, mu_x, mu_lo, p512, wcomb, ones_blk)


def _mlstm_dir(d, q, k, v, gates, gbias, c_ref, n_ref, m_ref, reverse, p):
    T = q.shape[0]
    mm = functools.partial(_mm, pa=p, pb=p)
    incl, _ = _masks(T, reverse)
    edge = 0 if reverse else T - 1
    g = gates + gbias
    lane = lax.broadcasted_iota(jnp.int32, g.shape, 1)
    g = jnp.where(lane < 2 * MLSTM_HEADS, g, _log_sigmoid(g))
    gt = g.T
    ones = _ones_mask(incl)
    b_col = _mm(ones, g, NN, 1, 3)
    b_row = _mm(gt, ones, NT, 3, 1)
    outs = []
    for h in range(MLSTM_HEADS):
        sl = slice(h * MLSTM_HD, (h + 1) * MLSTM_HD)
        ci = d * MLSTM_HEADS + h
        cf = 2 * MLSTM_HEADS + ci
        qh = q[:, sl] * (MLSTM_HD ** -0.5)
        kh = k[:, sl]
        vh = v[:, sl]
        bc = b_col[:, cf:cf + 1]
        br = b_row[cf:cf + 1, :]
        li_c = g[:, ci:ci + 1]
        li_r = gt[ci:ci + 1, :]
        st = d * MLSTM_HEADS + h
        C0 = c_ref[st]
        n0 = n_ref[st, 0:1, :]
        m0 = m_ref[st, 0:1, 0:1]
        logD = jnp.where(incl, bc - br + li_r, -jnp.inf)
        m_prev = bc + m0
        m = jnp.maximum(m_prev, jnp.max(logD, axis=-1, keepdims=True))
        s = mm(qh, kh, NT) * jnp.exp(logD - m)
        inter = jnp.exp(m_prev - m)
        num = mm(s, vh) + inter * mm(qh, C0, NT)
        den = jnp.sum(s, axis=-1, keepdims=True) + inter * jnp.sum(qh * n0, axis=-1, keepdims=True)
        outs.append(num / jnp.maximum(jnp.abs(den), jnp.exp(-m)))
        bT = bc[edge:edge + 1, :]
        gsum = bT - bc + li_c
        m_new = jnp.maximum(bT + m0, jnp.max(gsum, axis=0, keepdims=True))
        wts = jnp.exp(gsum - m_new)
        cd = jnp.exp(bT + m0 - m_new)
        c_ref[st] = cd * C0 + mm(wts * vh, kh, TN)
        n_ref[st, 0:1, :] = cd * n0 + jnp.sum(wts * kh, axis=0, keepdims=True)
        m_ref[st] = jnp.broadcast_to(m_new, m_ref.shape[1:])
    return jnp.concatenate(outs, axis=1)


def _mlstm_kernel(qf_ref, kf_ref, vf_ref, gf_ref, qb_ref, kb_ref, vb_ref, gb_ref, gbias_ref,
                  of_ref, ob_ref, c_ref, n_ref, m_ref, *, prec):
    i = pl.program_id(1)

    @pl.when(i == 0)
    def _():
        c_ref[...] = jnp.zeros_like(c_ref)
        n_ref[...] = jnp.zeros_like(n_ref)
        m_ref[...] = jnp.zeros_like(m_ref)

    gbias = gbias_ref[...]
    of_ref[0] = _mlstm_dir(0, qf_ref[0], kf_ref[0], vf_ref[0], gf_ref[0], gbias, c_ref, n_ref, m_ref,
                           False, prec)
    ob_ref[0] = _mlstm_dir(1, qb_ref[0], kb_ref[0], vb_ref[0], gb_ref[0], gbias, c_ref, n_ref, m_ref,
                           True, prec)


def _mlstm(proj, gbias, n_lat, n_ctx, prec=1):
    b, ltot, _ = proj.shape
    T = T_MLSTM
    fwd = functools.partial(_fwd_blk, n_lat=n_lat, n_ctx=n_ctx)
    bwd = functools.partial(_bwd_blk, n_lat=n_lat, n_ctx=n_ctx)
    spec = lambda width, col, order: pl.BlockSpec((1, T, width), lambda bi, i: (bi, order(i), col))
    out = jax.ShapeDtypeStruct((b, ltot, MLSTM_W), F32)
    nst = 2 * MLSTM_HEADS
    return pl.pallas_call(
        functools.partial(_mlstm_kernel, prec=prec),
        grid=(b, n_lat + n_ctx),
        in_specs=[
            spec(MLSTM_W, EV_Q, fwd), spec(MLSTM_W, EV_K, fwd), spec(MLSTM_W, EV_V, fwd),
            spec(128, EV_GATES, fwd),
            spec(MLSTM_W, EV_Q, bwd), spec(MLSTM_W, EV_K, bwd), spec(MLSTM_W, EV_V, bwd),
            spec(128, EV_GATES, bwd),
            pl.BlockSpec((1, 128), lambda bi, i: (0, 0)),
        ],
        out_specs=[
            pl.BlockSpec((1, T, MLSTM_W), lambda bi, i: (bi, fwd(i), 0)),
            pl.BlockSpec((1, T, MLSTM_W), lambda bi, i: (bi, bwd(i), 0)),
        ],
        out_shape=[out, out],
        scratch_shapes=[
            pltpu.VMEM((nst, MLSTM_HD, MLSTM_HD), F32),
            pltpu.VMEM((nst, 8, MLSTM_HD), F32),
            pltpu.VMEM((nst, 8, 128), F32),
        ],
        compiler_params=pltpu.CompilerParams(
            dimension_semantics=("arbitrary", "arbitrary"), vmem_limit_bytes=VMEM_LIMIT),
        name="mlstm_chunked",
    )(proj, proj, proj, proj, proj, proj, proj, proj, gbias)


def _hgrn_dir(d, q, fpre, vi, lb, s_ref, reverse, p):
    T = q.shape[0]
    mm = functools.partial(_mm, pa=p, pb=p)
    incl, _ = _masks(T, reverse)
    edge = 0 if reverse else T - 1
    f = lb + (1.0 - lb) * _sigmoid(fpre)
    kk = 1.0 - f
    qs = _silu(q)
    bc = _mm(_ones_mask(incl), jnp.log(f), NN, 1, 3)
    bmid = bc[T // 2:T // 2 + 1, :]
    bT = bc[edge:edge + 1, :]
    qt = qs * jnp.exp(bc - bmid)
    kt = kk * jnp.exp(bmid - bc)
    qd = qs * jnp.exp(bc)
    kd = kk * jnp.exp(bT - bc)
    eT = jnp.exp(bT)
    outs = []
    for h in range(HGRN_HEADS):
        sl = slice(h * HGRN_FD, (h + 1) * HGRN_FD)
        st = d * HGRN_HEADS + h
        S = s_ref[st]
        A = jnp.where(incl, mm(qt[:, sl], kt[:, sl], NT), 0.0)
        outs.append(mm(A, vi[:, sl]) + mm(qd[:, sl], S, NT))
        s_ref[st] = S * eT[:, sl] + mm(vi[:, sl], kd[:, sl], TN)
    return jnp.concatenate(outs, axis=1)


def _hgrn_kernel(qf_ref, ff_ref, if_ref, qb_ref, fb_ref, ib_ref, lb_ref, of_ref, ob_ref, s_ref,
                 *, layer, prec):
    i = pl.program_id(1)

    @pl.when(i == 0)
    def _():
        s_ref[...] = jnp.zeros_like(s_ref)

    raw = lb_ref[...]
    e = jnp.exp(raw - jnp.max(raw, axis=0, keepdims=True))
    sm = e / jnp.sum(e, axis=0, keepdims=True)
    lb = jnp.sum(sm[:layer + 1], axis=0, keepdims=True) - sm[0:1]
    of_ref[0] = _hgrn_dir(0, qf_ref[0], ff_ref[0], if_ref[0], lb, s_ref, False, prec)
    ob_ref[0] = _hgrn_dir(1, qb_ref[0], fb_ref[0], ib_ref[0], lb, s_ref, True, prec)


def _hgrn(proj, hgrn_lb, layer, n_lat, n_ctx, prec=1):
    b, ltot, _ = proj.shape
    T = T_HGRN
    fwd = functools.partial(_fwd_blk, n_lat=n_lat, n_ctx=n_ctx)
    bwd = functools.partial(_bwd_blk, n_lat=n_lat, n_ctx=n_ctx)
    spec = lambda col, order: pl.BlockSpec((1, T, HGRN_W), lambda bi, i: (bi, order(i), col))
    out = jax.ShapeDtypeStruct((b, ltot, HGRN_W), F32)
    return pl.pallas_call(
        functools.partial(_hgrn_kernel, layer=layer, prec=prec),
        grid=(b, n_lat + n_ctx),
        in_specs=[
            spec(OD_Q, fwd), spec(OD_F, fwd), spec(OD_I, fwd),
            spec(OD_Q, bwd), spec(OD_F + 1, bwd), spec(OD_I, bwd),
            pl.BlockSpec(hgrn_lb.shape, lambda bi, i: (0, 0)),
        ],
        out_specs=[
            pl.BlockSpec((1, T, HGRN_W), lambda bi, i: (bi, fwd(i), 0)),
            pl.BlockSpec((1, T, HGRN_W), lambda bi, i: (bi, bwd(i), 0)),
        ],
        out_shape=[out, out],
        scratch_shapes=[pltpu.VMEM((2 * HGRN_HEADS, HGRN_FD, HGRN_FD), F32)],
        compiler_params=pltpu.CompilerParams(
            dimension_semantics=("arbitrary", "arbitrary"), vmem_limit_bytes=VMEM_LIMIT),
        name="hgrn2_chunked",
    )(proj, proj, proj, proj, proj, proj, hgrn_lb)


def _lru_conv(x, halo_prev, halo_next, has_prev, has_next, cw, cb):
    T = x.shape[0]
    row = lax.broadcasted_iota(jnp.int32, x.shape, 0)
    hp = jnp.where(has_prev, halo_prev, 0.0)
    hn = jnp.where(has_next, halo_next, 0.0)
    x_m1 = jnp.where(row == 0, hp[7:8, :], pltpu.roll(x, 1, 0))
    x_m2 = jnp.where(row == 0, hp[6:7, :], jnp.where(row == 1, hp[7:8, :], pltpu.roll(x, 2, 0)))
    x_p1 = jnp.where(row == T - 1, hn[0:1, :], pltpu.roll(x, T - 1, 0))
    return x_m2 * cw[0:1, :] + x_m1 * cw[1:2, :] + x * cw[2:3, :] + x_p1 * cw[3:4, :] + cb


def _lru_dir(d, xc, wg, bias, sp_lam, h_ref, reverse, p):
    T = xc.shape[0]
    gates = _sigmoid(_mm(xc, wg, NN, p, p) + bias)
    r = gates[:, :LRU_W]
    ig = gates[:, LRU_W:]
    log_a = -LRU_C * r * sp_lam
    a = jnp.exp(log_a)
    th = jnp.tanh(log_a)
    u = jnp.sqrt(-2.0 * th / (1.0 - th)) * (ig * xc)
    row = lax.broadcasted_iota(jnp.int32, xc.shape, 0)
    k = 1
    while k < T:
        if reverse:
            ok = row < T - k
            a_n = jnp.where(ok, pltpu.roll(a, T - k, 0), 1.0)
            u_n = jnp.where(ok, pltpu.roll(u, T - k, 0), 0.0)
        else:
            ok = row >= k
            a_n = jnp.where(ok, pltpu.roll(a, k, 0), 1.0)
            u_n = jnp.where(ok, pltpu.roll(u, k, 0), 0.0)
        u = a * u_n + u
        a = a * a_n
        k *= 2
    hs = a * h_ref[d, 0:1, :] + u
    edge = 0 if reverse else T - 1
    h_ref[d, 0:1, :] = hs[edge:edge + 1, :]
    return hs


def _lru_kernel(xf_ref, pf_ref, nf_ref, xb_ref, pb_ref, nb_ref, cw_ref, cb_ref, wg_ref, bias_ref,
                lam_ref, of_ref, ob_ref, h_ref, *, n_lat, n_ctx, prec):
    i = pl.program_id(1)

    @pl.when(i == 0)
    def _():
        h_ref[...] = jnp.zeros_like(h_ref)

    cw = cw_ref[...]
    cb = cb_ref[...]
    for d, (x_ref, p_ref, n_ref, o_ref) in enumerate(
            ((xf_ref, pf_ref, nf_ref, of_ref), (xb_ref, pb_ref, nb_ref, ob_ref))):
        blk = _bwd_blk(i, n_lat, n_ctx) if d else _fwd_blk(i, n_lat, n_ctx)
        has_prev = (blk != 0) & (blk != n_lat)
        has_next = (blk != n_lat - 1) & (blk != n_lat + n_ctx - 1)
        xc = _lru_conv(x_ref[0], p_ref[0], n_ref[0], has_prev, has_next, cw, cb)
        sp_lam = _softplus(-lam_ref[d])
        o_ref[0] = _lru_dir(d, xc, wg_ref[d], bias_ref[d], sp_lam, h_ref, bool(d), prec)


def _lru(proj, conv_w, conv_b, wg, bias, lam, n_lat, n_ctx, prec=1):
    b, ltot, _ = proj.shape
    T = T_LRU
    r8 = T // 8
    n8 = ltot // 8
    fwd = functools.partial(_fwd_blk, n_lat=n_lat, n_ctx=n_ctx)
    bwd = functools.partial(_bwd_blk, n_lat=n_lat, n_ctx=n_ctx)
    col8 = OD_XB
    cur = lambda order: pl.BlockSpec((1, T, LRU_W), lambda bi, i: (bi, order(i), OD_XB))
    prev = lambda order: pl.BlockSpec(
        (1, 8, LRU_W), lambda bi, i: (bi, jnp.maximum(order(i) * r8 - 1, 0), col8))
    nxt = lambda order: pl.BlockSpec(
        (1, 8, LRU_W), lambda bi, i: (bi, jnp.minimum((order(i) + 1) * r8, n8 - 1), col8))
    full = lambda shape: pl.BlockSpec(shape, lambda bi, i: (0,) * len(shape))
    out = jax.ShapeDtypeStruct((b, ltot, LRU_W), F32)
    return pl.pallas_call(
        functools.partial(_lru_kernel, n_lat=n_lat, n_ctx=n_ctx, prec=prec),
        grid=(b, n_lat + n_ctx),
        in_specs=[
            cur(fwd), prev(fwd), nxt(fwd), cur(bwd), prev(bwd), nxt(bwd),
            full((4, LRU_W)), full((1, LRU_W)), full((2, LRU_W, 2 * LRU_W)), full((2, 1, 2 * LRU_W)),
            full((2, 1, LRU_W)),
        ],
        out_specs=[
            pl.BlockSpec((1, T, LRU_W), lambda bi, i: (bi, fwd(i), 0)),
            pl.BlockSpec((1, T, LRU_W), lambda bi, i: (bi, bwd(i), 0)),
        ],
        out_shape=[out, out],
        scratch_shapes=[pltpu.VMEM((2, 8, LRU_W), F32)],
        compiler_params=pltpu.CompilerParams(
            dimension_semantics=("arbitrary", "arbitrary"), vmem_limit_bytes=VMEM_LIMIT),
        name="rglru_scan",
    )(proj, proj, proj, proj, proj, proj, conv_w, conv_b, wg, bias, lam)


def _head_rms(x, heads, width):
    outs = []
    for h in range(heads):
        xh = x[:, h * width:(h + 1) * width]
        outs.append(xh * lax.rsqrt(jnp.mean(xh * xh, axis=-1, keepdims=True) + EPS))
    return jnp.concatenate(outs, axis=1)


def _out_even_kernel(h_ref, rf_ref, rb_ref, gd_ref, mf_ref, mb_ref, op_ref, g2_ref, ng_ref, w_ref,
                     mod_ref, o_ref):
    m = mod_ref[0]
    g = _mm(_sigmoid(gd_ref[0]), g2_ref[...], NN, 2, 2)
    y1 = (rf_ref[0] + rb_ref[0]) * g
    y2 = _head_rms(mf_ref[0] + mb_ref[0], MLSTM_HEADS, MLSTM_HD) * ng_ref[...] * _sigmoid(op_ref[0])
    y = jnp.concatenate([y1, y2], axis=1).astype(BF16)
    o_ref[0] = h_ref[0] + m[2:3] * jnp.dot(y, w_ref[...], preferred_element_type=F32)


def _out_even(h_all, rf, rb, proj, mf, mb, g2, ng, w_out, mod, n_lat_tiles):
    b, ltot, d = h_all.shape
    tok = lambda width, col=0: pl.BlockSpec((1, TM, width), lambda i, t: (i, t, col))
    full = lambda shape: pl.BlockSpec(shape, lambda i, t: (0,) * len(shape))
    return pl.pallas_call(
        _out_even_kernel,
        grid=(b, ltot // TM),
        in_specs=[
            tok(d), tok(RWKV_W), tok(RWKV_W), tok(128, EV_GD), tok(MLSTM_W), tok(MLSTM_W),
            tok(MLSTM_W, EV_O), full((GATE_LORA, RWKV_W)), full((1, MLSTM_W)), full((d, d)),
            pl.BlockSpec((1, 6, d), lambda i, t: (jnp.where(t >= n_lat_tiles, 4, i), 0, 0)),
        ],
        out_specs=tok(d),
        out_shape=jax.ShapeDtypeStruct(h_all.shape, F32),
        input_output_aliases={0: 0},
        compiler_params=pltpu.CompilerParams(vmem_limit_bytes=VMEM_LIMIT),
        name="out_proj_even",
    )(h_all, rf, rb, proj, mf, mb, proj, g2, ng, w_out, mod)


def _gelu_tanh(x):
    return 0.5 * x * (1.0 + jnp.tanh(0.7978845608028654 * (x + 0.044715 * x * x * x)))


def _gather_rows(ref):
    return jnp.concatenate([ref[0, :, r, :] for r in range(ref.shape[2])], axis=0)


def _out_odd_kernel(h_ref, hf_ref, hb_ref, g_ref, lf_ref, lb_ref, gb_ref, ng_ref, w_ref, mod_ref, o_ref):
    m = mod_ref[0]
    hsum = _gather_rows(hf_ref) + _gather_rows(hb_ref)
    y1 = _head_rms(hsum, HGRN_HEADS, HGRN_FD) * ng_ref[...] * _silu(_gather_rows(g_ref))
    y2 = (_gather_rows(lf_ref) + _gather_rows(lb_ref)) * _gelu_tanh(_gather_rows(gb_ref))
    y = jnp.concatenate([y1, y2], axis=1).astype(BF16)
    o_ref[0] = h_ref[0] + m[2:3] * jnp.dot(y, w_ref[...], preferred_element_type=F32)


def _out_odd(h_all, hf, hb, proj, lf, lb, ng, w_out, mod, rows):
    b, ltot, d = h_all.shape
    tr = TM_ODD // GRID_W
    cm = lambda a: a.reshape(b, ltot // rows, rows, a.shape[-1])
    col = lambda width, c=0: pl.BlockSpec((1, GRID_W, tr, width), lambda i, t: (i, 0, t, c))
    full = lambda shape: pl.BlockSpec(shape, lambda i, t: (0,) * len(shape))
    hspec = pl.BlockSpec((1, TM_ODD, d), lambda i, t: (i, t, 0))
    return pl.pallas_call(
        _out_odd_kernel,
        grid=(b, rows * GRID_W // TM_ODD),
        in_specs=[
            hspec, col(HGRN_W), col(HGRN_W), col(HGRN_W, OD_G), col(LRU_W), col(LRU_W), col(LRU_W, OD_GB),
            full((1, HGRN_W)), full((d, d)),
            pl.BlockSpec((1, 6, d), lambda i, t: (i, 0, 0)),
        ],
        out_specs=hspec,
        out_shape=jax.ShapeDtypeStruct((b, rows * GRID_W, d), F32),
        compiler_params=pltpu.CompilerParams(vmem_limit_bytes=VMEM_LIMIT),
        name="out_proj_odd",
    )(h_all, cm(hf), cm(hb), cm(proj), cm(lf), cm(lb), cm(proj), ng, w_out, mod)


def _ffn_kernel(h_ref, g_ref, mod_ref, w1_ref, w3_ref, w2_ref, fg_ref, o_ref, *, final):
    m = mod_ref[0]
    h = h_ref[0]
    v = _norm_mod(h, g_ref[...], m[3:4], m[4:5]).astype(BF16)
    a = jnp.dot(v, w1_ref[...], preferred_element_type=F32)
    c = jnp.dot(v, w3_ref[...], preferred_element_type=F32)
    hid = (_silu(a) * c).astype(BF16)
    out = h + m[5:6] * jnp.dot(hid, w2_ref[...], preferred_element_type=F32)
    if final:
        ms = jnp.mean(out * out, axis=-1, keepdims=True)
        out = out * lax.rsqrt(ms + EPS) * fg_ref[...]
    o_ref[0] = out


def _ffn(h_all, g, mod, w1, w3, w2, final_g, n_lat_tiles, final):
    b, ltot, d = h_all.shape
    ff = w1.shape[1]
    nt = n_lat_tiles if final else ltot // TM
    tok = pl.BlockSpec((1, TM, d), lambda i, t: (i, t, 0))
    full = lambda shape: pl.BlockSpec(shape, lambda i, t: (0,) * len(shape))
    return pl.pallas_call(
        functools.partial(_ffn_kernel, final=final),
        grid=(b, nt),
        in_specs=[
            tok, full((1, d)),
            pl.BlockSpec((1, 6, d), lambda i, t: (jnp.where(t >= n_lat_tiles, 4, i), 0, 0)),
            full((d, ff)), full((d, ff)), full((ff, d)), full((1, d)),
        ],
        out_specs=tok,
        out_shape=jax.ShapeDtypeStruct((b, nt * TM, d), F32),
        compiler_params=pltpu.CompilerParams(vmem_limit_bytes=VMEM_LIMIT),
        name="ffn_final" if final else "ffn",
    )(h_all, g, mod, w1, w3, w2, final_g)


def _pack_even_in_w(w):
    r, k, v, wd, ad, gd, q, mk, mv, mo, ip, fp = jnp.split(
        w, [512, 1024, 1536, 1664, 1792, 1920, 2432, 2944, 3456, 3968, 3976], axis=1)
    lora = jnp.concatenate([wd[:, :LORA], ad[:, :LORA], wd[:, LORA:], ad[:, LORA:]], axis=1)
    gates = jnp.concatenate([ip, fp, jnp.zeros((w.shape[0], 128 - 16), w.dtype)], axis=1)
    return jnp.concatenate([r, k, v, q, mk, mv, mo, lora, gd, gates], axis=1).astype(BF16)


def _pack_rwkv_params(mu, w0, w2, a0, a2, kk, ka, rk, lnw, lnb):
    mu_x = mu[:, None, :3 * RWKV_W]
    mu_lo = mu[:, None, 3 * RWKV_W:]
    p512 = jnp.stack([w0, a0, kk, ka, lnw, lnb, rk.reshape(2, RWKV_W), jnp.zeros_like(w0)], axis=1)
    z = jnp.zeros_like(w2)
    wcomb = jnp.concatenate([jnp.concatenate([w2, z], axis=2), jnp.concatenate([z, a2], axis=2)], axis=1)
    return mu_x, mu_lo, p512, wcomb


def _pack_lru_params(wa, ba, wi, bi):
    def dense(w):
        eye = jnp.eye(LRU_BLOCKS, dtype=w.dtype)
        return jnp.einsum('dhij,hg->dhigj', w, eye).reshape(2, LRU_W, LRU_W)
    wg = jnp.concatenate([dense(wa), dense(wi)], axis=2)
    bias = jnp.concatenate([ba, bi], axis=1)[:, None, :]
    return wg, bias


def kernel(x, c, ctx, c_ctx, norm1_g, norm2_g, mod_w, mod_b, ffn_w1, ffn_w3, ffn_w2, final_g, ev_in_w, ev_out_w, rwkv_mu, rwkv_w0, rwkv_w2, rwkv_a0, rwkv_a2, rwkv_kk, rwkv_ka, rwkv_rk, rwkv_lnw, rwkv_lnb, rwkv_g2, mlstm_bi, mlstm_bf, mlstm_ng, od_in_w, od_out_w, hgrn_lb, hgrn_ng, lru_conv_w, lru_conv_b, lru_wa, lru_ba, lru_wi, lru_bi, lru_lam):
    b, seq, d = x.shape
    n_ctx_tok = ctx.shape[1]
    rows = seq // GRID_W
    n_lat_tiles = seq // TM
    depth = mod_w.shape[0]

    c8 = jnp.concatenate([c, c_ctx[None, :], jnp.zeros((8 - b - 1, d), F32)], axis=0)
    mod = _modulation(c8, mod_w, mod_b).reshape(depth, 8, 6, d)
    h_all = jnp.concatenate([x, ctx], axis=1)

    for l in range(depth):
        j = l // 2
        last = l == depth - 1
        g1 = norm1_g[l][None, :]
        if l % 2 == 0:
            proj = _in_proj_even(h_all, g1, mod[l], _pack_even_in_w(ev_in_w[j]), n_lat_tiles)
            mu_x, mu_lo, p512, wcomb = _pack_rwkv_params(
                rwkv_mu[j], rwkv_w0[j], rwkv_w2[j], rwkv_a0[j], rwkv_a2[j], rwkv_kk[j], rwkv_ka[j],
                rwkv_rk[j], rwkv_lnw[j], rwkv_lnb[j])
            rf, rb = _rwkv(proj, mu_x, mu_lo, p512, wcomb, seq // T_RWKV, n_ctx_tok // T_RWKV)
            gbias = jnp.concatenate([mlstm_bi[j].reshape(-1), mlstm_bf[j].reshape(-1),
                                     jnp.zeros((128 - 4 * MLSTM_HEADS,), F32)])[None, :]
            mf, mb = _mlstm(proj, gbias, seq // T_MLSTM, n_ctx_tok // T_MLSTM)
            h_all = _out_even(h_all, rf, rb, proj, mf, mb, rwkv_g2[j], mlstm_ng[j][None, :],
                              ev_out_w[j].astype(BF16), mod[l], n_lat_tiles)
        else:
            assert last, "the odd layer's output kernel keeps latent tokens only"
            proj = _in_proj_odd(h_all, g1, mod[l], od_in_w[j].astype(BF16), rows)
            hf, hb = _hgrn(proj, hgrn_lb, l, seq // T_HGRN, n_ctx_tok // T_HGRN)
            wg, bias = _pack_lru_params(lru_wa[j], lru_ba[j], lru_wi[j], lru_bi[j])
            lf, lb = _lru(proj, lru_conv_w[j], lru_conv_b[j][None, :], wg, bias, lru_lam[j][:, None, :],
                          seq // T_LRU, n_ctx_tok // T_LRU)
            h_all = _out_odd(h_all, hf, hb, proj, lf, lb, hgrn_ng[j][None, :], od_out_w[j].astype(BF16),
                             mod[l], rows)
        h_all = _ffn(h_all, norm2_g[l][None, :], mod[l], ffn_w1[l].astype(BF16), ffn_w3[l].astype(BF16),
                     ffn_w2[l].astype(BF16), final_g[None, :], n_lat_tiles, last)
    return h_all
```

```python
import functools

import jax
import jax.numpy as jnp
from jax import lax
from jax.experimental import pallas as pl
from jax.experimental.pallas import tpu as pltpu

F32 = jnp.float32
BF16 = jnp.bfloat16

D_MODEL = 1024
GRID_W = 64
D_FF = 2816
EPS = 1e-6

RWKV_HEADS = 8
RWKV_HD = 64
RWKV_W = RWKV_HEADS * RWKV_HD
LORA = 64
GATE_LORA = 128
RWKV_GN_EPS = 64e-5
MLSTM_HEADS = 4
MLSTM_HD = 128
MLSTM_W = MLSTM_HEADS * MLSTM_HD
HGRN_HEADS = 4
HGRN_FD = 128
HGRN_W = HGRN_HEADS * HGRN_FD
LRU_W = 512
LRU_BLOCKS = 8
LRU_BD = LRU_W // LRU_BLOCKS
LRU_C = 8.0

T_RWKV = 64
T_MLSTM = 128
T_HGRN = 64
T_LRU = 256
TM = 256
TM_ODD = 8 * GRID_W

EV_N = 4096
EV_RKV, EV_Q, EV_K, EV_V, EV_O = 0, 3, 4, 5, 6
EV_LORA, EV_GD, EV_GATES = 28, 30, 31
OD_N = 3584
OD_Q, OD_F, OD_I, OD_G, OD_XB, OD_GB = 0, 1, 3, 4, 5, 6

VMEM_LIMIT = 56 * 1024 * 1024

NT = ((1,), (1,))
NN = ((1,), (0,))
TN = ((0,), (0,))


def _split(x, n):
    parts = []
    r = x
    for i in range(n):
        p = r.astype(BF16)
        parts.append(p)
        if i + 1 < n:
            r = r - p.astype(F32)
    return parts


def _mm(a, b, dims=NN, pa=1, pb=1):
    dn = (dims, ((), ()))
    ap = _split(a, pa) if a.dtype != BF16 else [a]
    bp = _split(b, pb) if b.dtype != BF16 else [b]
    n = max(len(ap), len(bp))
    acc = None
    for i, x in enumerate(ap):
        for j, y in enumerate(bp):
            if i + j < n:
                t = lax.dot_general(x, y, dn, preferred_element_type=F32)
                acc = t if acc is None else acc + t
    return acc


def _sigmoid(x):
    return 1.0 / (1.0 + jnp.exp(-x))


def _silu(x):
    return x * _sigmoid(x)


def _log_sigmoid(x):
    return jnp.minimum(x, 0.0) - jnp.log1p(jnp.exp(-jnp.abs(x)))


def _softplus(x):
    return jnp.maximum(x, 0.0) + jnp.log1p(jnp.exp(-jnp.abs(x)))


def _masks(T, reverse):
    ti = lax.broadcasted_iota(jnp.int32, (T, T), 0)
    si = lax.broadcasted_iota(jnp.int32, (T, T), 1)
    if reverse:
        return si >= ti, si > ti
    return si <= ti, si < ti


def _ones_mask(incl):
    return jnp.where(incl, 1.0, 0.0).astype(BF16)


def _shift_prev(x, carry_row, reverse):
    T = x.shape[0]
    row = lax.broadcasted_iota(jnp.int32, x.shape, 0)
    if reverse:
        return jnp.where(row == T - 1, carry_row, pltpu.roll(x, T - 1, 0))
    return jnp.where(row == 0, carry_row, pltpu.roll(x, 1, 0))


def _fwd_blk(i, n_lat, n_ctx):
    return jnp.where(i < n_ctx, n_lat + i, i - n_ctx)


def _bwd_blk(i, n_lat, n_ctx):
    return n_lat + n_ctx - 1 - i


def _mod_kernel(c_ref, w_ref, b_ref, o_ref):
    c = c_ref[...]
    o_ref[0] = _mm(_silu(c), w_ref[0], NN, 3, 3) + b_ref[0]


def _modulation(c8, mod_w, mod_b):
    depth, d, n = mod_w.shape
    return pl.pallas_call(
        _mod_kernel,
        grid=(depth, n // d),
        in_specs=[
            pl.BlockSpec((8, d), lambda l, j: (0, 0)),
            pl.BlockSpec((1, d, d), lambda l, j: (l, 0, j)),
            pl.BlockSpec((1, 1, d), lambda l, j: (l, 0, j)),
        ],
        out_specs=pl.BlockSpec((1, 8, d), lambda l, j: (l, 0, j)),
        out_shape=jax.ShapeDtypeStruct((depth, 8, n), F32),
        compiler_params=pltpu.CompilerParams(vmem_limit_bytes=VMEM_LIMIT),
        name="adaln_modulation",
    )(c8, mod_w, mod_b.reshape(depth, 1, n))


def _norm_mod(x, g, shift, scale):
    ms = jnp.mean(x * x, axis=-1, keepdims=True)
    return (x * lax.rsqrt(ms + EPS) * g) * (1.0 + scale) + shift


def _lat_or_ctx(x_ref, c_ref, n_lat_tiles):
    return jnp.where(pl.program_id(1) < n_lat_tiles, x_ref[0], c_ref[0])


def _lat_ctx_specs(d, n_lat_tiles):
    return [
        pl.BlockSpec((1, TM, d), lambda i, t: (i, jnp.minimum(t, n_lat_tiles - 1), 0)),
        pl.BlockSpec((1, TM, d), lambda i, t: (i, 0, 0)),
    ]


def _in_even_kernel(x_ref, c_ref, g_ref, mod_ref, w_ref, o_ref, *, n_lat_tiles):
    m = mod_ref[0]
    u = _norm_mod(_lat_or_ctx(x_ref, c_ref, n_lat_tiles), g_ref[...], m[0:1], m[1:2])
    o_ref[0] = jnp.dot(u.astype(BF16), w_ref[...], preferred_element_type=F32)


def _in_proj_even(x, ctx, g, mod, w, n_lat_tiles):
    b, seq, d = x.shape
    assert ctx.shape[1] == TM
    n = w.shape[1]
    nt = n_lat_tiles + 1
    return pl.pallas_call(
        functools.partial(_in_even_kernel, n_lat_tiles=n_lat_tiles),
        grid=(b, nt),
        in_specs=_lat_ctx_specs(d, n_lat_tiles) + [
            pl.BlockSpec((1, d), lambda i, t: (0, 0)),
            pl.BlockSpec((1, 6, d), lambda i, t: (jnp.where(t >= n_lat_tiles, 4, i), 0, 0)),
            pl.BlockSpec((d, n), lambda i, t: (0, 0)),
        ],
        out_specs=pl.BlockSpec((1, TM, n), lambda i, t: (i, t, 0)),
        out_shape=jax.ShapeDtypeStruct((b, nt * TM, n), F32),
        compiler_params=pltpu.CompilerParams(vmem_limit_bytes=VMEM_LIMIT),
        name="in_proj_even",
    )(x, ctx, g, mod, w)


def _in_odd_kernel(x_ref, g_ref, mod_ref, w_ref, *rest, scatter):
    o_ref = rest[-1]
    m = mod_ref[0]
    if scatter:
        x = jnp.concatenate([x_ref[0, :, w, :] for w in range(GRID_W)], axis=0)
    else:
        x = x_ref[0]
    u = _norm_mod(x, g_ref[...], m[0:1], m[1:2])
    y = jnp.dot(u.astype(BF16), w_ref[...], preferred_element_type=F32)
    if scatter:
        o_ref[0] = y.reshape(o_ref.shape[1:])
    else:
        for j in range(o_ref.shape[1]):
            o_ref[0, j] = y[j * o_ref.shape[2]:(j + 1) * o_ref.shape[2], :]


def _in_proj_odd(h_all, g, mod, w, rows):
    b, ltot, d = h_all.shape
    n = w.shape[1]
    lat = rows * GRID_W
    n_ctx_slots = (ltot - lat) // rows
    tr = TM_ODD // GRID_W
    out_shape = jax.ShapeDtypeStruct((b, GRID_W + n_ctx_slots, rows, n), F32)
    common = [
        pl.BlockSpec((1, d), lambda i, t: (0, 0)),
        None,
        pl.BlockSpec((d, n), lambda i, t: (0, 0)),
    ]
    lat_specs = list(common)
    lat_specs[1] = pl.BlockSpec((1, 6, d), lambda i, t: (i, 0, 0))
    proj = pl.pallas_call(
        functools.partial(_in_odd_kernel, scatter=True),
        grid=(b, lat // TM_ODD),
        in_specs=[pl.BlockSpec((1, tr, GRID_W, d), lambda i, t: (i, t, 0, 0))] + lat_specs,
        out_specs=pl.BlockSpec((1, GRID_W, tr, n), lambda i, t: (i, 0, t, 0)),
        out_shape=out_shape,
        compiler_params=pltpu.CompilerParams(vmem_limit_bytes=VMEM_LIMIT),
        name="in_proj_odd",
    )(h_all.reshape(b, ltot // GRID_W, GRID_W, d), g, mod, w)
    ctx_tok = ltot - lat
    ctx_specs = list(common)
    ctx_specs[1] = pl.BlockSpec((1, 6, d), lambda i, t: (4, 0, 0))
    return pl.pallas_call(
        functools.partial(_in_odd_kernel, scatter=False),
        grid=(b, 1),
        in_specs=[pl.BlockSpec((1, ctx_tok, d), lambda i, t: (i, lat // ctx_tok, 0))] + ctx_specs
        + [pl.BlockSpec(memory_space=pl.ANY)],
        out_specs=pl.BlockSpec((1, n_ctx_slots, rows, n), lambda i, t: (i, GRID_W // n_ctx_slots, 0, 0)),
        out_shape=out_shape,
        input_output_aliases={4: 0},
        compiler_params=pltpu.CompilerParams(vmem_limit_bytes=VMEM_LIMIT),
        name="in_proj_odd_ctx",
    )(h_all, g, mod, w, proj).reshape(b, ltot, n)


def _rwkv_prep(x, lo, carry_x, carry_lo, mu_x, mu_lo, p512, wcomb, ones_blk, reverse):
    T = x.shape[0]
    incl, _ = _masks(T, reverse)
    edge = 0 if reverse else T - 1
    xs = x + mu_x * (_shift_prev(x, carry_x, reverse) - x)
    los = lo + mu_lo * (_shift_prev(lo, carry_lo, reverse) - lo)
    r = xs[:, :RWKV_W]
    k = xs[:, RWKV_W:2 * RWKV_W]
    v = xs[:, 2 * RWKV_W:]
    w0, a0, k_k, k_a = (p512[i:i + 1, :] for i in range(4))
    lane = lax.broadcasted_iota(jnp.int32, los.shape, 1)
    z = jnp.where(lane < LORA, jnp.tanh(los), los)
    wa = _mm(z, wcomb, NN, 2, 2)
    w = -_softplus(-(w0 + wa[:, :RWKV_W])) - 0.5
    a = _sigmoid(a0 + wa[:, RWKV_W:])
    logdec = -jnp.exp(w)
    kk = k * k_k
    ss = _mm(kk * kk, ones_blk, NN, 2, 1)
    kk = kk / jnp.maximum(jnp.sqrt(ss), 1e-12)
    k2 = k * (1.0 + (a - 1.0) * k_a)
    cl = _mm(_ones_mask(incl), logdec, NN, 1, 3)
    e_pos = jnp.exp(cl)
    e_neg = jnp.exp(-cl)
    return dict(
        A=kk * jnp.exp(cl - logdec), B=kk * a * e_neg, K=k2 * e_neg, R=r * e_pos, V=v,
        gT=e_pos[edge:edge + 1, :], r=r, k2=k2,
        carry_x=x[edge:edge + 1, :], carry_lo=lo[edge:edge + 1, :])


def _rwkv_post(o, u, p512, ones_blk):
    ln_w, ln_b, r_k = (p512[i:i + 1, :] for i in range(4, 7))
    inv_n = 1.0 / RWKV_HD
    mean = _mm(o, ones_blk, NN, 2, 1) * inv_n
    oc = o - mean
    var = _mm(oc * oc, ones_blk, NN, 2, 1) * inv_n
    gn = oc * lax.rsqrt(var + RWKV_GN_EPS) * ln_w + ln_b
    bonus = _mm(u['r'] * u['k2'] * r_k, ones_blk, NN, 2, 1) * u['V']
    return gn + bonus


def _rwkv_chunks(units, states, T, p):
    mm = functools.partial(_mm, pa=p, pb=p)
    jobs = []
    for ui, (u, reverse) in enumerate(units):
        incl, strict = _masks(T, reverse)
        for h in range(RWKV_HEADS):
            sl = slice(h * RWKV_HD, (h + 1) * RWKV_HD)
            jobs.append(dict(
                ui=ui, incl=incl, strict=strict,
                AR=jnp.concatenate([u['A'][:, sl], u['R'][:, sl]], axis=0),
                B=u['B'][:, sl], K=u['K'][:, sl], V=u['V'][:, sl], gT=u['gT'][:, sl],
                S=states[ui][h]))
    for j in jobs:
        j['GB'] = mm(j['AR'], j['B'], NT)
    for j in jobs:
        j['GK'] = mm(j['AR'], j['K'], NT)
    for j in jobs:
        j['SS'] = mm(j['AR'], j['S'], NT)
    for j in jobs:
        both = jnp.concatenate([j['strict'], j['incl']], axis=0)
        j['LV'] = mm(jnp.where(both, j['GK'], 0.0), j['V'])
        j['N'] = jnp.where(j['strict'], j['GB'][:T], 0.0)
    ri = lax.broadcasted_iota(jnp.int32, (T, T), 0)
    ci = lax.broadcasted_iota(jnp.int32, (T, T), 1)
    rc = ri ^ ci
    eye = jnp.where(rc == 0, 1.0, 0.0)
    for j in jobs:
        j['D'] = eye - jnp.where(rc == 1, j['N'], 0.0)
    s = 2
    while s < T:
        level = (rc >= s) & (rc < 2 * s)
        for j in jobs:
            j['LD'] = mm(jnp.where(level, j['N'], 0.0), j['D'])
        for j in jobs:
            j['D'] = j['D'] - mm(j['D'], j['LD'])
        s *= 2
    for j in jobs:
        j['X'] = mm(j['D'], j['SS'][:T] + j['LV'][:T])
    for j in jobs:
        prb = jnp.where(j['incl'], j['GB'][T:], 0.0)
        j['O'] = j['SS'][T:] + j['LV'][T:] - mm(prb, j['X'])
    for j in jobs:
        dS = mm(jnp.concatenate([j['V'], -j['X']], axis=0), jnp.concatenate([j['K'], j['B']], axis=0), TN)
        j['S_new'] = (j['S'] + dS) * j['gT']
    outs, new_states = [], []
    for ui in range(len(units)):
        mine = [j for j in jobs if j['ui'] == ui]
        outs.append(jnp.concatenate([j['O'] for j in mine], axis=1))
        new_states.append([j['S_new'] for j in mine])
    return outs, new_states


def _rwkv_kernel(xf_ref, xb_ref, lf_ref, lb_ref, mux_ref, mulo_ref, p512_ref, wcomb_ref, ones_ref,
                 of_ref, ob_ref, s_ref, cx_ref, cl_ref, *, n_ctx, prec):
    i = pl.program_id(1)
    T = xf_ref.shape[1]

    @pl.when(i == 0)
    def _():
        s_ref[...] = jnp.zeros_like(s_ref)

    @pl.when((i == 0) | (i == n_ctx))
    def _():
        cx_ref[...] = jnp.zeros_like(cx_ref)
        cl_ref[...] = jnp.zeros_like(cl_ref)

    ones_blk = ones_ref[...]
    refs = ((xf_ref, lf_ref, of_ref), (xb_ref, lb_ref, ob_ref))
    units, states, where = [], [], []
    for bi in range(xf_ref.shape[0]):
        for d, (x_ref, lo_ref, _) in enumerate(refs):
            u = _rwkv_prep(x_ref[bi], lo_ref[bi], cx_ref[bi, d, 0:1, :], cl_ref[bi, d, 0:1, :], mux_ref[d],
                           mulo_ref[d], p512_ref[d], wcomb_ref[d], ones_blk, bool(d))
            cx_ref[bi, d, 0:1, :] = u['carry_x']
            cl_ref[bi, d, 0:1, :] = u['carry_lo']
            units.append((u, bool(d)))
            states.append([s_ref[bi, d, h] for h in range(RWKV_HEADS)])
            where.append((bi, d))
    outs, new_states = _rwkv_chunks(units, states, T, prec)
    for (bi, d), (u, _), o, ns in zip(where, units, outs, new_states):
        for h in range(RWKV_HEADS):
            s_ref[bi, d, h] = ns[h]
        refs[d][2][bi] = _rwkv_post(o, u, p512_ref[d], ones_blk)


def _rwkv(proj, mu_x, mu_lo, p512, wcomb, n_lat, n_ctx, prec=1, bb=1):
    b, ltot, _ = proj.shape
    T = T_RWKV
    ones_blk = jnp.kron(jnp.eye(RWKV_HEADS, dtype=F32), jnp.ones((RWKV_HD, RWKV_HD), F32)).astype(BF16)
    fwd = functools.partial(_fwd_blk, n_lat=n_lat, n_ctx=n_ctx)
    bwd = functools.partial(_bwd_blk, n_lat=n_lat, n_ctx=n_ctx)
    full = lambda shape: pl.BlockSpec(shape, lambda bi, i: (0,) * len(shape))
    out = jax.ShapeDtypeStruct((b, ltot, RWKV_W), F32)
    return pl.pallas_call(
        functools.partial(_rwkv_kernel, n_ctx=n_ctx, prec=prec),
        grid=(b // bb, n_lat + n_ctx),
        in_specs=[
            pl.BlockSpec((bb, T, 3 * RWKV_W), lambda bi, i: (bi, fwd(i), EV_RKV)),
            pl.BlockSpec((bb, T, 3 * RWKV_W), lambda bi, i: (bi, bwd(i), EV_RKV)),
            pl.BlockSpec((bb, T, 128), lambda bi, i: (bi, fwd(i), EV_LORA)),
            pl.BlockSpec((bb, T, 128), lambda bi, i: (bi, bwd(i), EV_LORA + 1)),
            full((2, 1, 3 * RWKV_W)), full((2, 1, 128)), full((2, 8, RWKV_W)),
            full((2, 128, 2 * RWKV_W)), full((RWKV_W, RWKV_W)),
        ],
        out_specs=[
            pl.BlockSpec((bb, T, RWKV_W), lambda bi, i: (bi, fwd(i), 0)),
            pl.BlockSpec((bb, T, RWKV_W), lambda bi, i: (bi, bwd(i), 0)),
        ],
        out_shape=[out, out],
        scratch_shapes=[
            pltpu.VMEM((bb, 2, RWKV_HEADS, RWKV_HD, RWKV_HD), F32),
            pltpu.VMEM((bb, 2, 8, 3 * RWKV_W), F32),
            pltpu.VMEM((bb, 2, 8, 128), F32),
        ],
        compiler_params=pltpu.CompilerParams(
            dimension_semantics=("arbitrary", "arbitrary"), vmem_limit_bytes=VMEM_LIMIT),
        name="rwkv7_chunked",
    )(proj, proj, proj, proj, mu_x, mu_lo, p512, wcomb, ones_blk)


def _mlstm_chunks(units, gbias, c_ref, n_ref, m_ref, p):
    mm = functools.partial(_mm, pa=p, pb=p)
    jobs = []
    for base, d, q, k, v, gates, reverse in units:
        T = q.shape[0]
        incl, _ = _masks(T, reverse)
        edge = 0 if reverse else T - 1
        g = gates + gbias
        lane = lax.broadcasted_iota(jnp.int32, g.shape, 1)
        g = jnp.where(lane < 2 * MLSTM_HEADS, g, _log_sigmoid(g))
        gt = g.T
        ones = _ones_mask(incl)
        b_col = _mm(ones, g, NN, 1, 3)
        b_row = _mm(gt, ones, NT, 3, 1)
        for h in range(MLSTM_HEADS):
            sl = slice(h * MLSTM_HD, (h + 1) * MLSTM_HD)
            ci = d * MLSTM_HEADS + h
            cf = 2 * MLSTM_HEADS + ci
            st = base + h
            jobs.append(dict(
                st=st, incl=incl, edge=edge,
                q=q[:, sl] * (MLSTM_HD ** -0.5), k=k[:, sl], v=v[:, sl],
                bc=b_col[:, cf:cf + 1], br=b_row[cf:cf + 1, :], li_c=g[:, ci:ci + 1], li_r=gt[ci:ci + 1, :],
                C0=c_ref[st], n0=n_ref[st, 0:1, :], m0=m_ref[st, 0:1, 0:1]))
    for j in jobs:
        j['qk'] = mm(j['q'], j['k'], NT)
    for j in jobs:
        j['qC'] = mm(j['q'], j['C0'], NT)
    for j in jobs:
        logD = jnp.where(j['incl'], j['bc'] - j['br'] + j['li_r'], -jnp.inf)
        m_prev = j['bc'] + j['m0']
        j['m'] = jnp.maximum(m_prev, jnp.max(logD, axis=-1, keepdims=True))
        j['s'] = j['qk'] * jnp.exp(logD - j['m'])
        j['inter'] = jnp.exp(m_prev - j['m'])
        bT = j['bc'][j['edge']:j['edge'] + 1, :]
        gsum = bT - j['bc'] + j['li_c']
        j['m_new'] = jnp.maximum(bT + j['m0'], jnp.max(gsum, axis=0, keepdims=True))
        j['wts'] = jnp.exp(gsum - j['m_new'])
        j['cd'] = jnp.exp(bT + j['m0'] - j['m_new'])
    for j in jobs:
        j['sv'] = mm(j['s'], j['v'])
    ones_tt = jnp.ones((jobs[0]['s'].shape[1], MLSTM_HD), BF16)
    for j in jobs:
        j['rs'] = mm(j['s'], ones_tt)
    for j in jobs:
        j['qn'] = mm(j['q'], jnp.broadcast_to(j['n0'], (MLSTM_HD, MLSTM_HD)), NT)
    for j in jobs:
        j['dC'] = mm(j['wts'] * j['v'], j['k'], TN)
    outs = []
    for j in jobs:
        num = j['sv'] + j['inter'] * j['qC']
        den = j['rs'] + j['inter'] * j['qn']
        outs.append(num / jnp.maximum(jnp.abs(den), jnp.exp(-j['m'])))
        st = j['st']
        c_ref[st] = j['cd'] * j['C0'] + j['dC']
        n_ref[st, 0:1, :] = j['cd'] * j['n0'] + jnp.sum(j['wts'] * j['k'], axis=0, keepdims=True)
        m_ref[st] = jnp.broadcast_to(j['m_new'], m_ref.shape[1:])
    return [jnp.concatenate(outs[u * MLSTM_HEADS:(u + 1) * MLSTM_HEADS], axis=1) for u in range(len(units))]


def _mlstm_kernel(qf_ref, kf_ref, vf_ref, gf_ref, qb_ref, kb_ref, vb_ref, gb_ref, gbias_ref,
                  of_ref, ob_ref, c_ref, n_ref, m_ref, *, prec):
    i = pl.program_id(1)

    @pl.when(i == 0)
    def _():
        c_ref[...] = jnp.zeros_like(c_ref)
        n_ref[...] = jnp.zeros_like(n_ref)
        m_ref[...] = jnp.zeros_like(m_ref)

    refs = ((qf_ref, kf_ref, vf_ref, gf_ref, of_ref), (qb_ref, kb_ref, vb_ref, gb_ref, ob_ref))
    units, where = [], []
    for bi in range(qf_ref.shape[0]):
        for d, (q_ref, k_ref, v_ref, g_ref, o_ref) in enumerate(refs):
            units.append(((bi * 2 + d) * MLSTM_HEADS, d, q_ref[bi], k_ref[bi], v_ref[bi], g_ref[bi], bool(d)))
            where.append((o_ref, bi))
    outs = _mlstm_chunks(units, gbias_ref[...], c_ref, n_ref, m_ref, prec)
    for (o_ref, bi), o in zip(where, outs):
        o_ref[bi] = o


def _mlstm(proj, gbias, n_lat, n_ctx, prec=1, bb=1):
    b, ltot, _ = proj.shape
    T = T_MLSTM
    fwd = functools.partial(_fwd_blk, n_lat=n_lat, n_ctx=n_ctx)
    bwd = functools.partial(_bwd_blk, n_lat=n_lat, n_ctx=n_ctx)
    spec = lambda width, col, order: pl.BlockSpec((bb, T, width), lambda bi, i: (bi, order(i), col))
    out = jax.ShapeDtypeStruct((b, ltot, MLSTM_W), F32)
    nst = bb * 2 * MLSTM_HEADS
    return pl.pallas_call(
        functools.partial(_mlstm_kernel, prec=prec),
        grid=(b // bb, n_lat + n_ctx),
        in_specs=[
            spec(MLSTM_W, EV_Q, fwd), spec(MLSTM_W, EV_K, fwd), spec(MLSTM_W, EV_V, fwd),
            spec(128, EV_GATES, fwd),
            spec(MLSTM_W, EV_Q, bwd), spec(MLSTM_W, EV_K, bwd), spec(MLSTM_W, EV_V, bwd),
            spec(128, EV_GATES, bwd),
            pl.BlockSpec((1, 128), lambda bi, i: (0, 0)),
        ],
        out_specs=[
            pl.BlockSpec((bb, T, MLSTM_W), lambda bi, i: (bi, fwd(i), 0)),
            pl.BlockSpec((bb, T, MLSTM_W), lambda bi, i: (bi, bwd(i), 0)),
        ],
        out_shape=[out, out],
        scratch_shapes=[
            pltpu.VMEM((nst, MLSTM_HD, MLSTM_HD), F32),
            pltpu.VMEM((nst, 8, MLSTM_HD), F32),
            pltpu.VMEM((nst, 8, 128), F32),
        ],
        compiler_params=pltpu.CompilerParams(
            dimension_semantics=("arbitrary", "arbitrary"), vmem_limit_bytes=VMEM_LIMIT),
        name="mlstm_chunked",
    )(proj, proj, proj, proj, proj, proj, proj, proj, gbias)


def _hgrn_chunks(units, lb, s_ref, p):
    mm = functools.partial(_mm, pa=p, pb=p)
    jobs = []
    for base, q, fpre, vi, reverse in units:
        T = q.shape[0]
        incl, _ = _masks(T, reverse)
        edge = 0 if reverse else T - 1
        f = lb + (1.0 - lb) * _sigmoid(fpre)
        kk = 1.0 - f
        qs = _silu(q)
        bc = _mm(_ones_mask(incl), jnp.log(f), NN, 1, 3)
        bmid = bc[T // 2:T // 2 + 1, :]
        bT = bc[edge:edge + 1, :]
        qt = qs * jnp.exp(bc - bmid)
        kt = kk * jnp.exp(bmid - bc)
        qd = qs * jnp.exp(bc)
        kd = kk * jnp.exp(bT - bc)
        eT = jnp.exp(bT)
        for h in range(HGRN_HEADS):
            sl = slice(h * HGRN_FD, (h + 1) * HGRN_FD)
            jobs.append(dict(st=base + h, incl=incl, qt=qt[:, sl], kt=kt[:, sl], qd=qd[:, sl], kd=kd[:, sl],
                             v=vi[:, sl], eT=eT[:, sl], S=s_ref[base + h]))
    for j in jobs:
        j['A'] = jnp.where(j['incl'], mm(j['qt'], j['kt'], NT), 0.0)
    for j in jobs:
        j['qS'] = mm(j['qd'], j['S'], NT)
    for j in jobs:
        j['o'] = mm(j['A'], j['v']) + j['qS']
    for j in jobs:
        s_ref[j['st']] = j['S'] * j['eT'] + mm(j['v'], j['kd'], TN)
    outs = [j['o'] for j in jobs]
    return [jnp.concatenate(outs[u * HGRN_HEADS:(u + 1) * HGRN_HEADS], axis=1) for u in range(len(units))]


def _hgrn_kernel(qf_ref, ff_ref, if_ref, qb_ref, fb_ref, ib_ref, lb_ref, of_ref, ob_ref, s_ref,
                 *, layer, prec):
    i = pl.program_id(1)

    @pl.when(i == 0)
    def _():
        s_ref[...] = jnp.zeros_like(s_ref)

    raw = lb_ref[...]
    e = jnp.exp(raw - jnp.max(raw, axis=0, keepdims=True))
    sm = e / jnp.sum(e, axis=0, keepdims=True)
    lb = jnp.sum(sm[:layer + 1], axis=0, keepdims=True) - sm[0:1]
    refs = ((qf_ref, ff_ref, if_ref, of_ref), (qb_ref, fb_ref, ib_ref, ob_ref))
    units, where = [], []
    for bi in range(qf_ref.shape[0]):
        for d, (q_ref, f_ref, v_ref, o_ref) in enumerate(refs):
            units.append(((bi * 2 + d) * HGRN_HEADS, q_ref[bi], f_ref[bi], v_ref[bi], bool(d)))
            where.append((o_ref, bi))
    outs = _hgrn_chunks(units, lb, s_ref, prec)
    for (o_ref, bi), o in zip(where, outs):
        o_ref[bi] = o


def _hgrn(proj, hgrn_lb, layer, n_lat, n_ctx, prec=1, bb=1):
    b, ltot, _ = proj.shape
    T = T_HGRN
    fwd = functools.partial(_fwd_blk, n_lat=n_lat, n_ctx=n_ctx)
    bwd = functools.partial(_bwd_blk, n_lat=n_lat, n_ctx=n_ctx)
    spec = lambda col, order: pl.BlockSpec((bb, T, HGRN_W), lambda bi, i: (bi, order(i), col))
    out = jax.ShapeDtypeStruct((b, ltot, HGRN_W), F32)
    return pl.pallas_call(
        functools.partial(_hgrn_kernel, layer=layer, prec=prec),
        grid=(b // bb, n_lat + n_ctx),
        in_specs=[
            spec(OD_Q, fwd), spec(OD_F, fwd), spec(OD_I, fwd),
            spec(OD_Q, bwd), spec(OD_F + 1, bwd), spec(OD_I, bwd),
            pl.BlockSpec(hgrn_lb.shape, lambda bi, i: (0, 0)),
        ],
        out_specs=[
            pl.BlockSpec((bb, T, HGRN_W), lambda bi, i: (bi, fwd(i), 0)),
            pl.BlockSpec((bb, T, HGRN_W), lambda bi, i: (bi, bwd(i), 0)),
        ],
        out_shape=[out, out],
        scratch_shapes=[pltpu.VMEM((bb * 2 * HGRN_HEADS, HGRN_FD, HGRN_FD), F32)],
        compiler_params=pltpu.CompilerParams(
            dimension_semantics=("arbitrary", "arbitrary"), vmem_limit_bytes=VMEM_LIMIT),
        name="hgrn2_chunked",
    )(proj, proj, proj, proj, proj, proj, hgrn_lb)


def _lru_conv(x, halo_prev, halo_next, has_prev, has_next, cw, cb):
    T = x.shape[0]
    row = lax.broadcasted_iota(jnp.int32, x.shape, 0)
    hp = jnp.where(has_prev, halo_prev, 0.0)
    hn = jnp.where(has_next, halo_next, 0.0)
    x_m1 = jnp.where(row == 0, hp[7:8, :], pltpu.roll(x, 1, 0))
    x_m2 = jnp.where(row == 0, hp[6:7, :], jnp.where(row == 1, hp[7:8, :], pltpu.roll(x, 2, 0)))
    x_p1 = jnp.where(row == T - 1, hn[0:1, :], pltpu.roll(x, T - 1, 0))
    return x_m2 * cw[0:1, :] + x_m1 * cw[1:2, :] + x * cw[2:3, :] + x_p1 * cw[3:4, :] + cb


def _lru_dir(d, xc, wg, bias, sp_lam, h_ref, reverse, p):
    T = xc.shape[0]
    gates = _sigmoid(_mm(xc, wg, NN, p, p) + bias)
    r = gates[:, :LRU_W]
    ig = gates[:, LRU_W:]
    log_a = -LRU_C * r * sp_lam
    a = jnp.exp(log_a)
    th = jnp.tanh(log_a)
    u = jnp.sqrt(-2.0 * th / (1.0 - th)) * (ig * xc)
    row = lax.broadcasted_iota(jnp.int32, xc.shape, 0)
    k = 1
    while k < T:
        if reverse:
            ok = row < T - k
            a_n = jnp.where(ok, pltpu.roll(a, T - k, 0), 1.0)
            u_n = jnp.where(ok, pltpu.roll(u, T - k, 0), 0.0)
        else:
            ok = row >= k
            a_n = jnp.where(ok, pltpu.roll(a, k, 0), 1.0)
            u_n = jnp.where(ok, pltpu.roll(u, k, 0), 0.0)
        u = a * u_n + u
        a = a * a_n
        k *= 2
    hs = a * h_ref[d, 0:1, :] + u
    edge = 0 if reverse else T - 1
    h_ref[d, 0:1, :] = hs[edge:edge + 1, :]
    return hs


def _lru_kernel(xf_ref, pf_ref, nf_ref, xb_ref, pb_ref, nb_ref, cw_ref, cb_ref, wg_ref, bias_ref,
                lam_ref, of_ref, ob_ref, h_ref, *, n_lat, n_ctx, prec):
    i = pl.program_id(1)

    @pl.when(i == 0)
    def _():
        h_ref[...] = jnp.zeros_like(h_ref)

    cw = cw_ref[...]
    cb = cb_ref[...]
    for d, (x_ref, p_ref, n_ref, o_ref) in enumerate(
            ((xf_ref, pf_ref, nf_ref, of_ref), (xb_ref, pb_ref, nb_ref, ob_ref))):
        blk = _bwd_blk(i, n_lat, n_ctx) if d else _fwd_blk(i, n_lat, n_ctx)
        has_prev = (blk != 0) & (blk != n_lat)
        has_next = (blk != n_lat - 1) & (blk != n_lat + n_ctx - 1)
        xc = _lru_conv(x_ref[0], p_ref[0], n_ref[0], has_prev, has_next, cw, cb)
        sp_lam = _softplus(-lam_ref[d])
        o_ref[0] = _lru_dir(d, xc, wg_ref[d], bias_ref[d], sp_lam, h_ref, bool(d), prec)


def _lru(proj, conv_w, conv_b, wg, bias, lam, n_lat, n_ctx, prec=1):
    b, ltot, _ = proj.shape
    T = T_LRU
    r8 = T // 8
    n8 = ltot // 8
    fwd = functools.partial(_fwd_blk, n_lat=n_lat, n_ctx=n_ctx)
    bwd = functools.partial(_bwd_blk, n_lat=n_lat, n_ctx=n_ctx)
    col8 = OD_XB
    cur = lambda order: pl.BlockSpec((1, T, LRU_W), lambda bi, i: (bi, order(i), OD_XB))
    prev = lambda order: pl.BlockSpec(
        (1, 8, LRU_W), lambda bi, i: (bi, jnp.maximum(order(i) * r8 - 1, 0), col8))
    nxt = lambda order: pl.BlockSpec(
        (1, 8, LRU_W), lambda bi, i: (bi, jnp.minimum((order(i) + 1) * r8, n8 - 1), col8))
    full = lambda shape: pl.BlockSpec(shape, lambda bi, i: (0,) * len(shape))
    out = jax.ShapeDtypeStruct((b, ltot, LRU_W), F32)
    return pl.pallas_call(
        functools.partial(_lru_kernel, n_lat=n_lat, n_ctx=n_ctx, prec=prec),
        grid=(b, n_lat + n_ctx),
        in_specs=[
            cur(fwd), prev(fwd), nxt(fwd), cur(bwd), prev(bwd), nxt(bwd),
            full((4, LRU_W)), full((1, LRU_W)), full((2, LRU_W, 2 * LRU_W)), full((2, 1, 2 * LRU_W)),
            full((2, 1, LRU_W)),
        ],
        out_specs=[
            pl.BlockSpec((1, T, LRU_W), lambda bi, i: (bi, fwd(i), 0)),
            pl.BlockSpec((1, T, LRU_W), lambda bi, i: (bi, bwd(i), 0)),
        ],
        out_shape=[out, out],
        scratch_shapes=[pltpu.VMEM((2, 8, LRU_W), F32)],
        compiler_params=pltpu.CompilerParams(
            dimension_semantics=("arbitrary", "arbitrary"), vmem_limit_bytes=VMEM_LIMIT),
        name="rglru_scan",
    )(proj, proj, proj, proj, proj, proj, conv_w, conv_b, wg, bias, lam)


def _head_rms(x, heads, width):
    outs = []
    for h in range(heads):
        xh = x[:, h * width:(h + 1) * width]
        outs.append(xh * lax.rsqrt(jnp.mean(xh * xh, axis=-1, keepdims=True) + EPS))
    return jnp.concatenate(outs, axis=1)


def _ffn_rows(h, m, g, w1_ref, w3_ref, w2_ref):
    v = _norm_mod(h, g, m[3:4], m[4:5]).astype(BF16)
    a = jnp.dot(v, w1_ref[...], preferred_element_type=F32)
    c = jnp.dot(v, w3_ref[...], preferred_element_type=F32)
    hid = (_silu(a) * c).astype(BF16)
    return h + m[5:6] * jnp.dot(hid, w2_ref[...], preferred_element_type=F32)


def _resident(shape):
    return pl.BlockSpec(shape, lambda i, t: (0,) * len(shape), pipeline_mode=pl.Buffered(1))


def _mix_ffn_even_kernel(x_ref, c_ref, rf_ref, rb_ref, gd_ref, mf_ref, mb_ref, op_ref, g2_ref, ng_ref, w_ref,
                         n2_ref, mod_ref, w1_ref, w3_ref, w2_ref, o_ref, *, n_lat_tiles):
    m = mod_ref[0]
    g = _mm(_sigmoid(gd_ref[0]), g2_ref[...], NN, 2, 2)
    y1 = (rf_ref[0] + rb_ref[0]) * g
    y2 = _head_rms(mf_ref[0] + mb_ref[0], MLSTM_HEADS, MLSTM_HD) * ng_ref[...] * _sigmoid(op_ref[0])
    y = jnp.concatenate([y1, y2], axis=1).astype(BF16)
    h = _lat_or_ctx(x_ref, c_ref, n_lat_tiles) + m[2:3] * jnp.dot(y, w_ref[...], preferred_element_type=F32)
    o_ref[0] = _ffn_rows(h, m, n2_ref[...], w1_ref, w3_ref, w2_ref)


def _mix_ffn_even(x, ctx, rf, rb, proj, mf, mb, g2, ng, w_out, n2g, mod, w1, w3, w2, n_lat_tiles):
    b, seq, d = x.shape
    ff = w1.shape[1]
    nt = n_lat_tiles + 1
    tok = lambda width, col=0: pl.BlockSpec((1, TM, width), lambda i, t: (i, t, col))
    return pl.pallas_call(
        functools.partial(_mix_ffn_even_kernel, n_lat_tiles=n_lat_tiles),
        grid=(b, nt),
        in_specs=_lat_ctx_specs(d, n_lat_tiles) + [
            tok(RWKV_W), tok(RWKV_W), tok(128, EV_GD), tok(MLSTM_W), tok(MLSTM_W), tok(MLSTM_W, EV_O),
            _resident((GATE_LORA, RWKV_W)), _resident((1, MLSTM_W)), _resident((d, d)), _resident((1, d)),
            pl.BlockSpec((1, 6, d), lambda i, t: (jnp.where(t >= n_lat_tiles, 4, i), 0, 0)),
            _resident((d, ff)), _resident((d, ff)), _resident((ff, d)),
        ],
        out_specs=tok(d),
        out_shape=jax.ShapeDtypeStruct((b, nt * TM, d), F32),
        compiler_params=pltpu.CompilerParams(vmem_limit_bytes=VMEM_LIMIT),
        name="mix_ffn_even",
    )(x, ctx, rf, rb, proj, mf, mb, proj, g2, ng, w_out, n2g, mod, w1, w3, w2)


def _gelu_tanh(x):
    return 0.5 * x * (1.0 + jnp.tanh(0.7978845608028654 * (x + 0.044715 * x * x * x)))


def _gather_rows(ref):
    return jnp.concatenate([ref[0, :, r, :] for r in range(ref.shape[2])], axis=0)


def _mix_ffn_odd_kernel(h_ref, hf_ref, hb_ref, g_ref, lf_ref, lb_ref, gb_ref, ng_ref, w_ref, n2_ref, mod_ref,
                        w1_ref, w3_ref, w2_ref, fg_ref, o_ref):
    m = mod_ref[0]
    hsum = _gather_rows(hf_ref) + _gather_rows(hb_ref)
    y1 = _head_rms(hsum, HGRN_HEADS, HGRN_FD) * ng_ref[...] * _silu(_gather_rows(g_ref))
    y2 = (_gather_rows(lf_ref) + _gather_rows(lb_ref)) * _gelu_tanh(_gather_rows(gb_ref))
    y = jnp.concatenate([y1, y2], axis=1).astype(BF16)
    h = h_ref[0] + m[2:3] * jnp.dot(y, w_ref[...], preferred_element_type=F32)
    for lo in range(0, h.shape[0], TM):
        out = _ffn_rows(h[lo:lo + TM], m, n2_ref[...], w1_ref, w3_ref, w2_ref)
        ms = jnp.mean(out * out, axis=-1, keepdims=True)
        o_ref[0, lo:lo + TM, :] = out * lax.rsqrt(ms + EPS) * fg_ref[...]


def _mix_ffn_odd(h_all, hf, hb, proj, lf, lb, ng, w_out, n2g, mod, w1, w3, w2, final_g, rows):
    b, ltot, d = h_all.shape
    ff = w1.shape[1]
    tr = TM_ODD // GRID_W
    cm = lambda a: a.reshape(b, ltot // rows, rows, a.shape[-1])
    col = lambda width, c=0: pl.BlockSpec((1, GRID_W, tr, width), lambda i, t: (i, 0, t, c))
    hspec = pl.BlockSpec((1, TM_ODD, d), lambda i, t: (i, t, 0))
    return pl.pallas_call(
        _mix_ffn_odd_kernel,
        grid=(b, rows * GRID_W // TM_ODD),
        in_specs=[
            hspec, col(HGRN_W), col(HGRN_W), col(HGRN_W, OD_G), col(LRU_W), col(LRU_W), col(LRU_W, OD_GB),
            _resident((1, HGRN_W)), _resident((d, d)), _resident((1, d)),
            pl.BlockSpec((1, 6, d), lambda i, t: (i, 0, 0)),
            _resident((d, ff)), _resident((d, ff)), _resident((ff, d)), _resident((1, d)),
        ],
        out_specs=hspec,
        out_shape=jax.ShapeDtypeStruct((b, rows * GRID_W, d), F32),
        compiler_params=pltpu.CompilerParams(vmem_limit_bytes=VMEM_LIMIT),
        name="mix_ffn_odd",
    )(h_all, cm(hf), cm(hb), cm(proj), cm(lf), cm(lb), cm(proj), ng, w_out, n2g, mod, w1, w3, w2, final_g)


def _pack_even_in_w(w):
    r, k, v, wd, ad, gd, q, mk, mv, mo, ip, fp = jnp.split(
        w, [512, 1024, 1536, 1664, 1792, 1920, 2432, 2944, 3456, 3968, 3976], axis=1)
    lora = jnp.concatenate([wd[:, :LORA], ad[:, :LORA], wd[:, LORA:], ad[:, LORA:]], axis=1)
    gates = jnp.concatenate([ip, fp, jnp.zeros((w.shape[0], 128 - 16), w.dtype)], axis=1)
    return jnp.concatenate([r, k, v, q, mk, mv, mo, lora, gd, gates], axis=1).astype(BF16)


def _pack_rwkv_params(mu, w0, w2, a0, a2, kk, ka, rk, lnw, lnb):
    mu_x = mu[:, None, :3 * RWKV_W]
    mu_lo = mu[:, None, 3 * RWKV_W:]
    p512 = jnp.stack([w0, a0, kk, ka, lnw, lnb, rk.reshape(2, RWKV_W), jnp.zeros_like(w0)], axis=1)
    z = jnp.zeros_like(w2)
    wcomb = jnp.concatenate([jnp.concatenate([w2, z], axis=2), jnp.concatenate([z, a2], axis=2)], axis=1)
    return mu_x, mu_lo, p512, wcomb


def _pack_lru_params(wa, ba, wi, bi):
    def dense(w):
        eye = jnp.eye(LRU_BLOCKS, dtype=w.dtype)
        return jnp.einsum('dhij,hg->dhigj', w, eye).reshape(2, LRU_W, LRU_W)
    wg = jnp.concatenate([dense(wa), dense(wi)], axis=2)
    bias = jnp.concatenate([ba, bi], axis=1)[:, None, :]
    return wg, bias


def kernel(x, c, ctx, c_ctx, norm1_g, norm2_g, mod_w, mod_b, ffn_w1, ffn_w3, ffn_w2, final_g, ev_in_w, ev_out_w, rwkv_mu, rwkv_w0, rwkv_w2, rwkv_a0, rwkv_a2, rwkv_kk, rwkv_ka, rwkv_rk, rwkv_lnw, rwkv_lnb, rwkv_g2, mlstm_bi, mlstm_bf, mlstm_ng, od_in_w, od_out_w, hgrn_lb, hgrn_ng, lru_conv_w, lru_conv_b, lru_wa, lru_ba, lru_wi, lru_bi, lru_lam):
    b, seq, d = x.shape
    n_ctx_tok = ctx.shape[1]
    rows = seq // GRID_W
    n_lat_tiles = seq // TM
    depth = mod_w.shape[0]

    assert depth == 2, "layer 0 reads x / ctx directly and the odd layer is the last one"
    c8 = jnp.concatenate([c, c_ctx[None, :], jnp.zeros((8 - b - 1, d), F32)], axis=0)
    mod = _modulation(c8, mod_w, mod_b).reshape(depth, 8, 6, d)

    h_all = None
    for l in range(depth):
        j = l // 2
        g1 = norm1_g[l][None, :]
        ffn_w = (ffn_w1[l].astype(BF16), ffn_w3[l].astype(BF16), ffn_w2[l].astype(BF16))
        if l % 2 == 0:
            proj = _in_proj_even(x, ctx, g1, mod[l], _pack_even_in_w(ev_in_w[j]), n_lat_tiles)
            mu_x, mu_lo, p512, wcomb = _pack_rwkv_params(
                rwkv_mu[j], rwkv_w0[j], rwkv_w2[j], rwkv_a0[j], rwkv_a2[j], rwkv_kk[j], rwkv_ka[j],
                rwkv_rk[j], rwkv_lnw[j], rwkv_lnb[j])
            rf, rb = _rwkv(proj, mu_x, mu_lo, p512, wcomb, seq // T_RWKV, n_ctx_tok // T_RWKV, bb=2)
            gbias = jnp.concatenate([mlstm_bi[j].reshape(-1), mlstm_bf[j].reshape(-1),
                                     jnp.zeros((128 - 4 * MLSTM_HEADS,), F32)])[None, :]
            mf, mb = _mlstm(proj, gbias, seq // T_MLSTM, n_ctx_tok // T_MLSTM, bb=2)
            h_all = _mix_ffn_even(x, ctx, rf, rb, proj, mf, mb, rwkv_g2[j], mlstm_ng[j][None, :],
                                  ev_out_w[j].astype(BF16), norm2_g[l][None, :], mod[l], *ffn_w, n_lat_tiles)
        else:
            proj = _in_proj_odd(h_all, g1, mod[l], od_in_w[j].astype(BF16), rows)
            hf, hb = _hgrn(proj, hgrn_lb, l, seq // T_HGRN, n_ctx_tok // T_HGRN, bb=2)
            wg, bias = _pack_lru_params(lru_wa[j], lru_ba[j], lru_wi[j], lru_bi[j])
            lf, lb = _lru(proj, lru_conv_w[j], lru_conv_b[j][None, :], wg, bias, lru_lam[j][:, None, :],
                          seq // T_LRU, n_ctx_tok // T_LRU)
            h_all = _mix_ffn_odd(h_all, hf, hb, proj, lf, lb, hgrn_ng[j][None, :], od_out_w[j].astype(BF16),
                                 norm2_g[l][None, :], mod[l], *ffn_w, final_g[None, :], rows)
    return h_all
```

```python
import functools

import jax
import jax.numpy as jnp
from jax import lax
from jax.experimental import pallas as pl
from jax.experimental.pallas import tpu as pltpu

F32 = jnp.float32
BF16 = jnp.bfloat16

D_MODEL = 1024
GRID_W = 64
D_FF = 2816
EPS = 1e-6

RWKV_HEADS = 8
RWKV_HD = 64
RWKV_W = RWKV_HEADS * RWKV_HD
LORA = 64
GATE_LORA = 128
RWKV_GN_EPS = 64e-5
MLSTM_HEADS = 4
MLSTM_HD = 128
MLSTM_W = MLSTM_HEADS * MLSTM_HD
HGRN_HEADS = 4
HGRN_FD = 128
HGRN_W = HGRN_HEADS * HGRN_FD
LRU_W = 512
LRU_BLOCKS = 8
LRU_BD = LRU_W // LRU_BLOCKS
LRU_C = 8.0
SUBLANES = 8

T_RWKV = 64
T_MLSTM = 128
T_HGRN = 64
T_LRU = 256
TM = 256
TM_ODD = 8 * GRID_W

EV_N = 4096
EV_RKV, EV_Q, EV_K, EV_V, EV_O = 0, 3, 4, 5, 6
EV_LORA, EV_GD, EV_GATES = 28, 30, 31
OD_N = 3584
OD_Q, OD_F, OD_I, OD_G, OD_XB, OD_GB = 0, 1, 3, 4, 5, 6

VMEM_LIMIT = 56 * 1024 * 1024

NT = ((1,), (1,))
NN = ((1,), (0,))
TN = ((0,), (0,))


def _split(x, n):
    parts = []
    r = x
    for i in range(n):
        p = r.astype(BF16)
        parts.append(p)
        if i + 1 < n:
            r = r - p.astype(F32)
    return parts


def _mm(a, b, dims=NN, pa=1, pb=1):
    dn = (dims, ((), ()))
    ap = _split(a, pa) if a.dtype != BF16 else [a]
    bp = _split(b, pb) if b.dtype != BF16 else [b]
    n = max(len(ap), len(bp))
    acc = None
    for i, x in enumerate(ap):
        for j, y in enumerate(bp):
            if i + j < n:
                t = lax.dot_general(x, y, dn, preferred_element_type=F32)
                acc = t if acc is None else acc + t
    return acc


def _sigmoid(x):
    return 1.0 / (1.0 + jnp.exp(-x))


def _silu(x):
    return x * _sigmoid(x)


def _log_sigmoid(x):
    return jnp.minimum(x, 0.0) - jnp.log1p(jnp.exp(-jnp.abs(x)))


def _softplus(x):
    return jnp.maximum(x, 0.0) + jnp.log1p(jnp.exp(-jnp.abs(x)))


def _masks(T, reverse):
    ti = lax.broadcasted_iota(jnp.int32, (T, T), 0)
    si = lax.broadcasted_iota(jnp.int32, (T, T), 1)
    if reverse:
        return si >= ti, si > ti
    return si <= ti, si < ti


def _ones_mask(incl):
    return jnp.where(incl, 1.0, 0.0).astype(BF16)


def _shift_prev(x, carry_row, reverse):
    T = x.shape[0]
    row = lax.broadcasted_iota(jnp.int32, x.shape, 0)
    if reverse:
        return jnp.where(row == T - 1, carry_row, pltpu.roll(x, T - 1, 0))
    return jnp.where(row == 0, carry_row, pltpu.roll(x, 1, 0))


def _fwd_blk(i, n_lat, n_ctx):
    return jnp.where(i < n_ctx, n_lat + i, i - n_ctx)


def _bwd_blk(i, n_lat, n_ctx):
    return n_lat + n_ctx - 1 - i


def _mod_kernel(c_ref, w_ref, b_ref, o_ref):
    c = c_ref[...]
    o_ref[0] = _mm(_silu(c), w_ref[0], NN, 3, 3) + b_ref[0]


def _modulation(c8, mod_w, mod_b):
    depth, d, n = mod_w.shape
    return pl.pallas_call(
        _mod_kernel,
        grid=(depth, n // d),
        in_specs=[
            pl.BlockSpec((8, d), lambda l, j: (0, 0)),
            pl.BlockSpec((1, d, d), lambda l, j: (l, 0, j)),
            pl.BlockSpec((1, 1, d), lambda l, j: (l, 0, j)),
        ],
        out_specs=pl.BlockSpec((1, 8, d), lambda l, j: (l, 0, j)),
        out_shape=jax.ShapeDtypeStruct((depth, 8, n), F32),
        compiler_params=pltpu.CompilerParams(vmem_limit_bytes=VMEM_LIMIT),
        name="adaln_modulation",
    )(c8, mod_w, mod_b.reshape(depth, 1, n))


def _norm_mod(x, g, shift, scale):
    ms = jnp.mean(x * x, axis=-1, keepdims=True)
    return (x * lax.rsqrt(ms + EPS) * g) * (1.0 + scale) + shift


def _lat_or_ctx(x_ref, c_ref, n_lat_tiles):
    return jnp.where(pl.program_id(1) < n_lat_tiles, x_ref[0], c_ref[0])


def _lat_ctx_specs(d, n_lat_tiles):
    return [
        pl.BlockSpec((1, TM, d), lambda i, t: (i, jnp.minimum(t, n_lat_tiles - 1), 0)),
        pl.BlockSpec((1, TM, d), lambda i, t: (i, 0, 0)),
    ]


def _in_even_kernel(x_ref, c_ref, g_ref, mod_ref, w_ref, o_ref, *, n_lat_tiles):
    m = mod_ref[0]
    u = _norm_mod(_lat_or_ctx(x_ref, c_ref, n_lat_tiles), g_ref[...], m[0:1], m[1:2])
    o_ref[0] = jnp.dot(u.astype(BF16), w_ref[...], preferred_element_type=F32)


def _in_proj_even(x, ctx, g, mod, w, n_lat_tiles):
    b, seq, d = x.shape
    assert ctx.shape[1] == TM
    n = w.shape[1]
    nt = n_lat_tiles + 1
    return pl.pallas_call(
        functools.partial(_in_even_kernel, n_lat_tiles=n_lat_tiles),
        grid=(b, nt),
        in_specs=_lat_ctx_specs(d, n_lat_tiles) + [
            pl.BlockSpec((1, d), lambda i, t: (0, 0)),
            pl.BlockSpec((1, 6, d), lambda i, t: (jnp.where(t >= n_lat_tiles, 4, i), 0, 0)),
            pl.BlockSpec((d, n), lambda i, t: (0, 0)),
        ],
        out_specs=pl.BlockSpec((1, TM, n), lambda i, t: (i, t, 0)),
        out_shape=jax.ShapeDtypeStruct((b, nt * TM, n), F32),
        compiler_params=pltpu.CompilerParams(vmem_limit_bytes=VMEM_LIMIT),
        name="in_proj_even",
    )(x, ctx, g, mod, w)


def _in_odd_kernel(x_ref, g_ref, mod_ref, w_ref, *rest, scatter):
    o_ref = rest[-1]
    m = mod_ref[0]
    if scatter:
        x = jnp.concatenate([x_ref[0, :, w, :] for w in range(GRID_W)], axis=0)
    else:
        x = x_ref[0]
    u = _norm_mod(x, g_ref[...], m[0:1], m[1:2])
    y = jnp.dot(u.astype(BF16), w_ref[...], preferred_element_type=F32)
    if scatter:
        o_ref[0] = y.reshape(o_ref.shape[1:])
    else:
        for j in range(o_ref.shape[1]):
            o_ref[0, j] = y[j * o_ref.shape[2]:(j + 1) * o_ref.shape[2], :]


def _in_proj_odd(h_all, g, mod, w, rows):
    b, ltot, d = h_all.shape
    n = w.shape[1]
    lat = rows * GRID_W
    n_ctx_slots = (ltot - lat) // rows
    tr = TM_ODD // GRID_W
    out_shape = jax.ShapeDtypeStruct((b, GRID_W + n_ctx_slots, rows, n), F32)
    common = [
        pl.BlockSpec((1, d), lambda i, t: (0, 0)),
        None,
        pl.BlockSpec((d, n), lambda i, t: (0, 0)),
    ]
    lat_specs = list(common)
    lat_specs[1] = pl.BlockSpec((1, 6, d), lambda i, t: (i, 0, 0))
    proj = pl.pallas_call(
        functools.partial(_in_odd_kernel, scatter=True),
        grid=(b, lat // TM_ODD),
        in_specs=[pl.BlockSpec((1, tr, GRID_W, d), lambda i, t: (i, t, 0, 0))] + lat_specs,
        out_specs=pl.BlockSpec((1, GRID_W, tr, n), lambda i, t: (i, 0, t, 0)),
        out_shape=out_shape,
        compiler_params=pltpu.CompilerParams(vmem_limit_bytes=VMEM_LIMIT),
        name="in_proj_odd",
    )(h_all.reshape(b, ltot // GRID_W, GRID_W, d), g, mod, w)
    ctx_tok = ltot - lat
    ctx_specs = list(common)
    ctx_specs[1] = pl.BlockSpec((1, 6, d), lambda i, t: (4, 0, 0))
    return pl.pallas_call(
        functools.partial(_in_odd_kernel, scatter=False),
        grid=(b, 1),
        in_specs=[pl.BlockSpec((1, ctx_tok, d), lambda i, t: (i, lat // ctx_tok, 0))] + ctx_specs
        + [pl.BlockSpec(memory_space=pl.ANY)],
        out_specs=pl.BlockSpec((1, n_ctx_slots, rows, n), lambda i, t: (i, GRID_W // n_ctx_slots, 0, 0)),
        out_shape=out_shape,
        input_output_aliases={4: 0},
        compiler_params=pltpu.CompilerParams(vmem_limit_bytes=VMEM_LIMIT),
        name="in_proj_odd_ctx",
    )(h_all, g, mod, w, proj).reshape(b, ltot, n)


def _head_sums(parts, ones_blk):
    T = parts[0].shape[0]
    s = _mm(jnp.concatenate(parts, axis=0), ones_blk, NN, 1, 1)
    return [s[i * T:(i + 1) * T] for i in range(len(parts))]


def _rwkv_shift(x, lo, carry_x, carry_lo, mu_x, mu_lo, reverse):
    xs = x + mu_x * (_shift_prev(x, carry_x, reverse) - x)
    los = lo + mu_lo * (_shift_prev(lo, carry_lo, reverse) - lo)
    lane = lax.broadcasted_iota(jnp.int32, los.shape, 1)
    return xs, jnp.where(lane < LORA, jnp.tanh(los), los)


def _rwkv_operands(xs, wa, ss, p512, reverse):
    T = xs.shape[0]
    incl, _ = _masks(T, reverse)
    edge = 0 if reverse else T - 1
    r = xs[:, :RWKV_W]
    k = xs[:, RWKV_W:2 * RWKV_W]
    v = xs[:, 2 * RWKV_W:]
    w0, a0, k_k, k_a = (p512[i:i + 1, :] for i in range(4))
    w = -_softplus(-(w0 + wa[:, :RWKV_W])) - 0.5
    a = _sigmoid(a0 + wa[:, RWKV_W:])
    logdec = -jnp.exp(w)
    kk = (k * k_k) / jnp.maximum(jnp.sqrt(ss), 1e-12)
    k2 = k * (1.0 + (a - 1.0) * k_a)
    cl = _mm(_ones_mask(incl), logdec, NN, 1, 3)
    e_pos = jnp.exp(cl)
    e_neg = jnp.exp(-cl)
    return dict(
        A=kk * jnp.exp(cl - logdec), B=kk * a * e_neg, K=k2 * e_neg, R=r * e_pos, V=v,
        gT=e_pos[edge:edge + 1, :], r=r, k2=k2)


def _rwkv_chunks(units, states, T, p):
    mm = functools.partial(_mm, pa=p, pb=p)
    jobs = []
    for ui, (u, reverse) in enumerate(units):
        incl, strict = _masks(T, reverse)
        for h in range(RWKV_HEADS):
            sl = slice(h * RWKV_HD, (h + 1) * RWKV_HD)
            jobs.append(dict(
                ui=ui, incl=incl, strict=strict,
                AR=jnp.concatenate([u['A'][:, sl], u['R'][:, sl]], axis=0),
                B=u['B'][:, sl], K=u['K'][:, sl], V=u['V'][:, sl], gT=u['gT'][:, sl],
                S=states[ui][h]))
    for j in jobs:
        j['GB'] = mm(j['AR'], j['B'], NT)
    for j in jobs:
        j['GK'] = mm(j['AR'], j['K'], NT)
    for j in jobs:
        j['SS'] = mm(j['AR'], j['S'], NT)
    for j in jobs:
        both = jnp.concatenate([j['strict'], j['incl']], axis=0)
        j['LV'] = mm(jnp.where(both, j['GK'], 0.0), j['V'])
        j['N'] = jnp.where(j['strict'], j['GB'][:T], 0.0)
    ri = lax.broadcasted_iota(jnp.int32, (T, T), 0)
    ci = lax.broadcasted_iota(jnp.int32, (T, T), 1)
    rc = ri ^ ci
    eye = jnp.where(rc == 0, 1.0, 0.0)
    for j in jobs:
        j['D'] = eye - jnp.where(rc == 1, j['N'], 0.0)
    s = 2
    while s < T:
        level = (rc >= s) & (rc < 2 * s)
        for j in jobs:
            j['LD'] = mm(jnp.where(level, j['N'], 0.0), j['D'])
        for j in jobs:
            j['D'] = j['D'] - mm(j['D'], j['LD'])
        s *= 2
    for j in jobs:
        j['X'] = mm(j['D'], j['SS'][:T] + j['LV'][:T])
    for j in jobs:
        prb = jnp.where(j['incl'], j['GB'][T:], 0.0)
        j['O'] = j['SS'][T:] + j['LV'][T:] - mm(prb, j['X'])
    for j in jobs:
        dS = mm(jnp.concatenate([j['V'], -j['X']], axis=0), jnp.concatenate([j['K'], j['B']], axis=0), TN)
        j['S_new'] = (j['S'] + dS) * j['gT']
    outs, new_states = [], []
    for ui in range(len(units)):
        mine = [j for j in jobs if j['ui'] == ui]
        outs.append(jnp.concatenate([j['O'] for j in mine], axis=1))
        new_states.append([j['S_new'] for j in mine])
    return outs, new_states


def _rwkv_kernel(xf_ref, xb_ref, lf_ref, lb_ref, mux_ref, mulo_ref, p512_ref, wcomb_ref, ones_ref,
                 of_ref, ob_ref, s_ref, cx_ref, cl_ref, *, n_ctx, prec):
    i = pl.program_id(1)
    T = xf_ref.shape[1]

    @pl.when(i == 0)
    def _():
        s_ref[...] = jnp.zeros_like(s_ref)

    @pl.when((i == 0) | (i == n_ctx))
    def _():
        cx_ref[...] = jnp.zeros_like(cx_ref)
        cl_ref[...] = jnp.zeros_like(cl_ref)

    ones_blk = ones_ref[...]
    refs = ((xf_ref, lf_ref, of_ref), (xb_ref, lb_ref, ob_ref))
    nb = xf_ref.shape[0]
    where = [(bi, d) for d in range(2) for bi in range(nb)]
    xs, zs = [], []
    for bi, d in where:
        x = refs[d][0][bi]
        lo = refs[d][1][bi]
        edge = 0 if d else T - 1
        xs_u, z_u = _rwkv_shift(x, lo, cx_ref[bi, d, 0:1, :], cl_ref[bi, d, 0:1, :], mux_ref[d],
                                mulo_ref[d], bool(d))
        cx_ref[bi, d, 0:1, :] = x[edge:edge + 1, :]
        cl_ref[bi, d, 0:1, :] = lo[edge:edge + 1, :]
        xs.append(xs_u)
        zs.append(z_u)
    was = []
    for d in range(2):
        wa = _mm(jnp.concatenate(zs[d * nb:(d + 1) * nb], axis=0), wcomb_ref[d], NN, 2, 2)
        was += [wa[bi * T:(bi + 1) * T] for bi in range(nb)]
    kk0 = [x_u[:, RWKV_W:2 * RWKV_W] * p512_ref[d][2:3, :] for x_u, (_, d) in zip(xs, where)]
    sss = _head_sums([k0 * k0 for k0 in kk0], ones_blk)
    units = [(_rwkv_operands(x_u, wa, ss, p512_ref[d], bool(d)), bool(d))
             for x_u, wa, ss, (_, d) in zip(xs, was, sss, where)]
    states = [[s_ref[bi, d, h] for h in range(RWKV_HEADS)] for bi, d in where]
    outs, new_states = _rwkv_chunks(units, states, T, prec)
    for (bi, d), ns in zip(where, new_states):
        for h in range(RWKV_HEADS):
            s_ref[bi, d, h] = ns[h]
    inv_n = 1.0 / RWKV_HD
    means = _head_sums(outs, ones_blk)
    ocs = [o - mean * inv_n for o, mean in zip(outs, means)]
    varis = _head_sums([oc * oc for oc in ocs], ones_blk)
    rks = _head_sums([u['r'] * u['k2'] * p512_ref[d][6:7, :] for (u, _), (_, d) in zip(units, where)], ones_blk)
    for (bi, d), (u, _), oc, var, rk in zip(where, units, ocs, varis, rks):
        ln_w, ln_b = p512_ref[d][4:5, :], p512_ref[d][5:6, :]
        gn = oc * lax.rsqrt(var * inv_n + RWKV_GN_EPS) * ln_w + ln_b
        refs[d][2][bi] = gn + rk * u['V']


def _rwkv(proj, mu_x, mu_lo, p512, wcomb, n_lat, n_ctx, prec=1, bb=1):
    b, ltot, _ = proj.shape
    T = T_RWKV
    ones_blk = jnp.kron(jnp.eye(RWKV_HEADS, dtype=F32), jnp.ones((RWKV_HD, RWKV_HD), F32)).astype(BF16)
    fwd = functools.partial(_fwd_blk, n_lat=n_lat, n_ctx=n_ctx)
    bwd = functools.partial(_bwd_blk, n_lat=n_lat, n_ctx=n_ctx)
    full = lambda shape: pl.BlockSpec(shape, lambda bi, i: (0,) * len(shape))
    out = jax.ShapeDtypeStruct((b, ltot, RWKV_W), F32)
    return pl.pallas_call(
        functools.partial(_rwkv_kernel, n_ctx=n_ctx, prec=prec),
        grid=(b // bb, n_lat + n_ctx),
        in_specs=[
            pl.BlockSpec((bb, T, 3 * RWKV_W), lambda bi, i: (bi, fwd(i), EV_RKV)),
            pl.BlockSpec((bb, T, 3 * RWKV_W), lambda bi, i: (bi, bwd(i), EV_RKV)),
            pl.BlockSpec((bb, T, 128), lambda bi, i: (bi, fwd(i), EV_LORA)),
            pl.BlockSpec((bb, T, 128), lambda bi, i: (bi, bwd(i), EV_LORA + 1)),
            full((2, 1, 3 * RWKV_W)), full((2, 1, 128)), full((2, 8, RWKV_W)),
            full((2, 128, 2 * RWKV_W)), full((RWKV_W, RWKV_W)),
        ],
        out_specs=[
            pl.BlockSpec((bb, T, RWKV_W), lambda bi, i: (bi, fwd(i), 0)),
            pl.BlockSpec((bb, T, RWKV_W), lambda bi, i: (bi, bwd(i), 0)),
        ],
        out_shape=[out, out],
        scratch_shapes=[
            pltpu.VMEM((bb, 2, RWKV_HEADS, RWKV_HD, RWKV_HD), F32),
            pltpu.VMEM((bb, 2, 8, 3 * RWKV_W), F32),
            pltpu.VMEM((bb, 2, 8, 128), F32),
        ],
        compiler_params=pltpu.CompilerParams(
            dimension_semantics=("arbitrary", "arbitrary"), vmem_limit_bytes=VMEM_LIMIT),
        name="rwkv7_chunked",
    )(proj, proj, proj, proj, mu_x, mu_lo, p512, wcomb, ones_blk)


def _mlstm_chunks(units, gbias, c_ref, n_ref, m_ref, p):
    mm = functools.partial(_mm, pa=p, pb=p)
    jobs = []
    for base, d, q, k, v, gates, reverse in units:
        T = q.shape[0]
        incl, _ = _masks(T, reverse)
        edge = 0 if reverse else T - 1
        g = gates + gbias
        lane = lax.broadcasted_iota(jnp.int32, g.shape, 1)
        g = jnp.where(lane < 2 * MLSTM_HEADS, g, _log_sigmoid(g))
        gt = g.T
        ones = _ones_mask(incl)
        b_col = _mm(ones, g, NN, 1, 3)
        b_row = _mm(gt, ones, NT, 3, 1)
        for h in range(MLSTM_HEADS):
            sl = slice(h * MLSTM_HD, (h + 1) * MLSTM_HD)
            ci = d * MLSTM_HEADS + h
            cf = 2 * MLSTM_HEADS + ci
            st = base + h
            jobs.append(dict(
                st=st, incl=incl, edge=edge,
                q=q[:, sl] * (MLSTM_HD ** -0.5), k=k[:, sl], v=v[:, sl],
                bc=b_col[:, cf:cf + 1], br=b_row[cf:cf + 1, :], li_c=g[:, ci:ci + 1], li_r=gt[ci:ci + 1, :],
                C0=c_ref[st], n0=n_ref[st, 0:1, :], m0=m_ref[st, 0:1, 0:1]))
    for j in jobs:
        j['qk'] = mm(j['q'], j['k'], NT)
    for j in jobs:
        j['qC'] = mm(j['q'], j['C0'], NT)
    for j in jobs:
        logD = jnp.where(j['incl'], j['bc'] - j['br'] + j['li_r'], -jnp.inf)
        m_prev = j['bc'] + j['m0']
        j['m'] = jnp.maximum(m_prev, jnp.max(logD, axis=-1, keepdims=True))
        j['s'] = j['qk'] * jnp.exp(logD - j['m'])
        j['inter'] = jnp.exp(m_prev - j['m'])
        bT = j['bc'][j['edge']:j['edge'] + 1, :]
        gsum = bT - j['bc'] + j['li_c']
        j['m_new'] = jnp.maximum(bT + j['m0'], jnp.max(gsum, axis=0, keepdims=True))
        j['wts'] = jnp.exp(gsum - j['m_new'])
        j['cd'] = jnp.exp(bT + j['m0'] - j['m_new'])
    for j in jobs:
        j['sv'] = mm(j['s'], j['v'])
    ones_tt = jnp.ones((jobs[0]['s'].shape[1], MLSTM_HD), BF16)
    for j in jobs:
        j['rs'] = mm(j['s'], ones_tt)
    for j in jobs:
        j['qn'] = mm(j['q'], jnp.broadcast_to(j['n0'], (MLSTM_HD, MLSTM_HD)), NT)
    for j in jobs:
        j['dC'] = mm(j['wts'] * j['v'], j['k'], TN)
    outs = []
    for j in jobs:
        num = j['sv'] + j['inter'] * j['qC']
        den = j['rs'] + j['inter'] * j['qn']
        outs.append(num / jnp.maximum(jnp.abs(den), jnp.exp(-j['m'])))
        st = j['st']
        c_ref[st] = j['cd'] * j['C0'] + j['dC']
        n_ref[st, 0:1, :] = j['cd'] * j['n0'] + jnp.sum(j['wts'] * j['k'], axis=0, keepdims=True)
        m_ref[st] = jnp.broadcast_to(j['m_new'], m_ref.shape[1:])
    return [jnp.concatenate(outs[u * MLSTM_HEADS:(u + 1) * MLSTM_HEADS], axis=1) for u in range(len(units))]


def _mlstm_kernel(qf_ref, kf_ref, vf_ref, gf_ref, qb_ref, kb_ref, vb_ref, gb_ref, gbias_ref,
                  of_ref, ob_ref, c_ref, n_ref, m_ref, *, prec):
    i = pl.program_id(1)

    @pl.when(i == 0)
    def _():
        c_ref[...] = jnp.zeros_like(c_ref)
        n_ref[...] = jnp.zeros_like(n_ref)
        m_ref[...] = jnp.zeros_like(m_ref)

    refs = ((qf_ref, kf_ref, vf_ref, gf_ref, of_ref), (qb_ref, kb_ref, vb_ref, gb_ref, ob_ref))
    units, where = [], []
    for bi in range(qf_ref.shape[0]):
        for d, (q_ref, k_ref, v_ref, g_ref, o_ref) in enumerate(refs):
            units.append(((bi * 2 + d) * MLSTM_HEADS, d, q_ref[bi], k_ref[bi], v_ref[bi], g_ref[bi], bool(d)))
            where.append((o_ref, bi))
    outs = _mlstm_chunks(units, gbias_ref[...], c_ref, n_ref, m_ref, prec)
    for (o_ref, bi), o in zip(where, outs):
        o_ref[bi] = o


def _mlstm(proj, gbias, n_lat, n_ctx, prec=1, bb=1):
    b, ltot, _ = proj.shape
    T = T_MLSTM
    fwd = functools.partial(_fwd_blk, n_lat=n_lat, n_ctx=n_ctx)
    bwd = functools.partial(_bwd_blk, n_lat=n_lat, n_ctx=n_ctx)
    spec = lambda width, col, order: pl.BlockSpec((bb, T, width), lambda bi, i: (bi, order(i), col))
    out = jax.ShapeDtypeStruct((b, ltot, MLSTM_W), F32)
    nst = bb * 2 * MLSTM_HEADS
    return pl.pallas_call(
        functools.partial(_mlstm_kernel, prec=prec),
        grid=(b // bb, n_lat + n_ctx),
        in_specs=[
            spec(MLSTM_W, EV_Q, fwd), spec(MLSTM_W, EV_K, fwd), spec(MLSTM_W, EV_V, fwd),
            spec(128, EV_GATES, fwd),
            spec(MLSTM_W, EV_Q, bwd), spec(MLSTM_W, EV_K, bwd), spec(MLSTM_W, EV_V, bwd),
            spec(128, EV_GATES, bwd),
            pl.BlockSpec((1, 128), lambda bi, i: (0, 0)),
        ],
        out_specs=[
            pl.BlockSpec((bb, T, MLSTM_W), lambda bi, i: (bi, fwd(i), 0)),
            pl.BlockSpec((bb, T, MLSTM_W), lambda bi, i: (bi, bwd(i), 0)),
        ],
        out_shape=[out, out],
        scratch_shapes=[
            pltpu.VMEM((nst, MLSTM_HD, MLSTM_HD), F32),
            pltpu.VMEM((nst, 8, MLSTM_HD), F32),
            pltpu.VMEM((nst, 8, 128), F32),
        ],
        compiler_params=pltpu.CompilerParams(
            dimension_semantics=("arbitrary", "arbitrary"), vmem_limit_bytes=VMEM_LIMIT),
        name="mlstm_chunked",
    )(proj, proj, proj, proj, proj, proj, proj, proj, gbias)


def _hgrn_chunks(units, lb, s_ref, p):
    mm = functools.partial(_mm, pa=p, pb=p)
    jobs = []
    for base, q, fpre, vi, reverse in units:
        T = q.shape[0]
        incl, _ = _masks(T, reverse)
        edge = 0 if reverse else T - 1
        f = lb + (1.0 - lb) * _sigmoid(fpre)
        kk = 1.0 - f
        qs = _silu(q)
        bc = _mm(_ones_mask(incl), jnp.log(f), NN, 1, 3)
        bmid = bc[T // 2:T // 2 + 1, :]
        bT = bc[edge:edge + 1, :]
        qt = qs * jnp.exp(bc - bmid)
        kt = kk * jnp.exp(bmid - bc)
        qd = qs * jnp.exp(bc)
        kd = kk * jnp.exp(bT - bc)
        eT = jnp.exp(bT)
        for h in range(HGRN_HEADS):
            sl = slice(h * HGRN_FD, (h + 1) * HGRN_FD)
            jobs.append(dict(st=base + h, incl=incl, qt=qt[:, sl], kt=kt[:, sl], qd=qd[:, sl], kd=kd[:, sl],
                             v=vi[:, sl], eT=eT[:, sl], S=s_ref[base + h]))
    for j in jobs:
        j['A'] = jnp.where(j['incl'], mm(j['qt'], j['kt'], NT), 0.0)
    for j in jobs:
        j['qS'] = mm(j['qd'], j['S'], NT)
    for j in jobs:
        j['o'] = mm(j['A'], j['v']) + j['qS']
    for j in jobs:
        s_ref[j['st']] = j['S'] * j['eT'] + mm(j['v'], j['kd'], TN)
    outs = [j['o'] for j in jobs]
    return [jnp.concatenate(outs[u * HGRN_HEADS:(u + 1) * HGRN_HEADS], axis=1) for u in range(len(units))]


def _hgrn_kernel(qf_ref, ff_ref, if_ref, qb_ref, fb_ref, ib_ref, lb_ref, of_ref, ob_ref, s_ref,
                 *, layer, prec):
    i = pl.program_id(1)

    @pl.when(i == 0)
    def _():
        s_ref[...] = jnp.zeros_like(s_ref)

    raw = lb_ref[...]
    e = jnp.exp(raw - jnp.max(raw, axis=0, keepdims=True))
    sm = e / jnp.sum(e, axis=0, keepdims=True)
    lb = jnp.sum(sm[:layer + 1], axis=0, keepdims=True) - sm[0:1]
    refs = ((qf_ref, ff_ref, if_ref, of_ref), (qb_ref, fb_ref, ib_ref, ob_ref))
    units, where = [], []
    for bi in range(qf_ref.shape[0]):
        for d, (q_ref, f_ref, v_ref, o_ref) in enumerate(refs):
            units.append(((bi * 2 + d) * HGRN_HEADS, q_ref[bi], f_ref[bi], v_ref[bi], bool(d)))
            where.append((o_ref, bi))
    outs = _hgrn_chunks(units, lb, s_ref, prec)
    for (o_ref, bi), o in zip(where, outs):
        o_ref[bi] = o


def _hgrn(proj, hgrn_lb, layer, n_lat, n_ctx, prec=1, bb=1):
    b, ltot, _ = proj.shape
    T = T_HGRN
    fwd = functools.partial(_fwd_blk, n_lat=n_lat, n_ctx=n_ctx)
    bwd = functools.partial(_bwd_blk, n_lat=n_lat, n_ctx=n_ctx)
    spec = lambda col, order: pl.BlockSpec((bb, T, HGRN_W), lambda bi, i: (bi, order(i), col))
    out = jax.ShapeDtypeStruct((b, ltot, HGRN_W), F32)
    return pl.pallas_call(
        functools.partial(_hgrn_kernel, layer=layer, prec=prec),
        grid=(b // bb, n_lat + n_ctx),
        in_specs=[
            spec(OD_Q, fwd), spec(OD_F, fwd), spec(OD_I, fwd),
            spec(OD_Q, bwd), spec(OD_F + 1, bwd), spec(OD_I, bwd),
            pl.BlockSpec(hgrn_lb.shape, lambda bi, i: (0, 0)),
        ],
        out_specs=[
            pl.BlockSpec((bb, T, HGRN_W), lambda bi, i: (bi, fwd(i), 0)),
            pl.BlockSpec((bb, T, HGRN_W), lambda bi, i: (bi, bwd(i), 0)),
        ],
        out_shape=[out, out],
        scratch_shapes=[pltpu.VMEM((bb * 2 * HGRN_HEADS, HGRN_FD, HGRN_FD), F32)],
        compiler_params=pltpu.CompilerParams(
            dimension_semantics=("arbitrary", "arbitrary"), vmem_limit_bytes=VMEM_LIMIT),
        name="hgrn2_chunked",
    )(proj, proj, proj, proj, proj, proj, hgrn_lb)


def _lru_conv(x, halo_prev, halo_next, has_prev, has_next, cw, cb):
    T = x.shape[0]
    row = lax.broadcasted_iota(jnp.int32, x.shape, 0)
    hp = jnp.where(has_prev, halo_prev, 0.0)
    hn = jnp.where(has_next, halo_next, 0.0)
    x_m1 = jnp.where(row == 0, hp[7:8, :], pltpu.roll(x, 1, 0))
    x_m2 = jnp.where(row == 0, hp[6:7, :], jnp.where(row == 1, hp[7:8, :], pltpu.roll(x, 2, 0)))
    x_p1 = jnp.where(row == T - 1, hn[0:1, :], pltpu.roll(x, T - 1, 0))
    return x_m2 * cw[0:1, :] + x_m1 * cw[1:2, :] + x * cw[2:3, :] + x_p1 * cw[3:4, :] + cb


def _lru_dir(d, xc, wg, bias, sp_lam, h_ref, reverse, p):
    T = xc.shape[0]
    gates = _sigmoid(_mm(xc, wg, NN, p, p) + bias)
    r = gates[:, :LRU_W]
    ig = gates[:, LRU_W:]
    log_a = -LRU_C * r * sp_lam
    a = jnp.exp(log_a)
    th = jnp.tanh(log_a)
    u = jnp.sqrt(-2.0 * th / (1.0 - th)) * (ig * xc)
    sub = lax.broadcasted_iota(jnp.int32, xc.shape, 0) & (SUBLANES - 1)
    k = 1
    while k < SUBLANES:
        ok = (sub < SUBLANES - k) if reverse else (sub >= k)
        shift = T - k if reverse else k
        a_n = jnp.where(ok, pltpu.roll(a, shift, 0), 1.0)
        u_n = jnp.where(ok, pltpu.roll(u, shift, 0), 0.0)
        u = a * u_n + u
        a = a * a_n
        k *= 2
    n_groups = T // SUBLANES
    last = 0 if reverse else SUBLANES - 1
    carry = jnp.broadcast_to(h_ref[d, 0:1, :], (SUBLANES, LRU_W))
    pieces = [None] * n_groups
    for g in (range(n_groups - 1, -1, -1) if reverse else range(n_groups)):
        rows = slice(g * SUBLANES, (g + 1) * SUBLANES)
        pieces[g] = a[rows] * carry + u[rows]
        carry = jnp.broadcast_to(pieces[g][last:last + 1, :], (SUBLANES, LRU_W))
    h_ref[d, 0:1, :] = carry[0:1, :]
    return jnp.concatenate(pieces, axis=0)


def _lru_kernel(xf_ref, pf_ref, nf_ref, xb_ref, pb_ref, nb_ref, cw_ref, cb_ref, wg_ref, bias_ref,
                lam_ref, of_ref, ob_ref, h_ref, *, n_lat, n_ctx, prec):
    i = pl.program_id(1)

    @pl.when(i == 0)
    def _():
        h_ref[...] = jnp.zeros_like(h_ref)

    cw = cw_ref[...]
    cb = cb_ref[...]
    for d, (x_ref, p_ref, n_ref, o_ref) in enumerate(
            ((xf_ref, pf_ref, nf_ref, of_ref), (xb_ref, pb_ref, nb_ref, ob_ref))):
        blk = _bwd_blk(i, n_lat, n_ctx) if d else _fwd_blk(i, n_lat, n_ctx)
        has_prev = (blk != 0) & (blk != n_lat)
        has_next = (blk != n_lat - 1) & (blk != n_lat + n_ctx - 1)
        xc = _lru_conv(x_ref[0], p_ref[0], n_ref[0], has_prev, has_next, cw, cb)
        sp_lam = _softplus(-lam_ref[d])
        o_ref[0] = _lru_dir(d, xc, wg_ref[d], bias_ref[d], sp_lam, h_ref, bool(d), prec)


def _lru(proj, conv_w, conv_b, wg, bias, lam, n_lat, n_ctx, prec=1):
    b, ltot, _ = proj.shape
    T = T_LRU
    r8 = T // 8
    n8 = ltot // 8
    fwd = functools.partial(_fwd_blk, n_lat=n_lat, n_ctx=n_ctx)
    bwd = functools.partial(_bwd_blk, n_lat=n_lat, n_ctx=n_ctx)
    col8 = OD_XB
    cur = lambda order: pl.BlockSpec((1, T, LRU_W), lambda bi, i: (bi, order(i), OD_XB))
    prev = lambda order: pl.BlockSpec(
        (1, 8, LRU_W), lambda bi, i: (bi, jnp.maximum(order(i) * r8 - 1, 0), col8))
    nxt = lambda order: pl.BlockSpec(
        (1, 8, LRU_W), lambda bi, i: (bi, jnp.minimum((order(i) + 1) * r8, n8 - 1), col8))
    full = lambda shape: pl.BlockSpec(shape, lambda bi, i: (0,) * len(shape))
    out = jax.ShapeDtypeStruct((b, ltot, LRU_W), F32)
    return pl.pallas_call(
        functools.partial(_lru_kernel, n_lat=n_lat, n_ctx=n_ctx, prec=prec),
        grid=(b, n_lat + n_ctx),
        in_specs=[
            cur(fwd), prev(fwd), nxt(fwd), cur(bwd), prev(bwd), nxt(bwd),
            full((4, LRU_W)), full((1, LRU_W)), full((2, LRU_W, 2 * LRU_W)), full((2, 1, 2 * LRU_W)),
            full((2, 1, LRU_W)),
        ],
        out_specs=[
            pl.BlockSpec((1, T, LRU_W), lambda bi, i: (bi, fwd(i), 0)),
            pl.BlockSpec((1, T, LRU_W), lambda bi, i: (bi, bwd(i), 0)),
        ],
        out_shape=[out, out],
        scratch_shapes=[pltpu.VMEM((2, 8, LRU_W), F32)],
        compiler_params=pltpu.CompilerParams(
            dimension_semantics=("arbitrary", "arbitrary"), vmem_limit_bytes=VMEM_LIMIT),
        name="rglru_scan",
    )(proj, proj, proj, proj, proj, proj, conv_w, conv_b, wg, bias, lam)


def _head_rms(x, heads, width):
    outs = []
    for h in range(heads):
        xh = x[:, h * width:(h + 1) * width]
        outs.append(xh * lax.rsqrt(jnp.mean(xh * xh, axis=-1, keepdims=True) + EPS))
    return jnp.concatenate(outs, axis=1)


def _ffn_rows(h, m, g, w1_ref, w3_ref, w2_ref):
    v = _norm_mod(h, g, m[3:4], m[4:5]).astype(BF16)
    a = jnp.dot(v, w1_ref[...], preferred_element_type=F32)
    c = jnp.dot(v, w3_ref[...], preferred_element_type=F32)
    hid = (_silu(a) * c).astype(BF16)
    return h + m[5:6] * jnp.dot(hid, w2_ref[...], preferred_element_type=F32)


def _resident(shape):
    return pl.BlockSpec(shape, lambda i, t: (0,) * len(shape), pipeline_mode=pl.Buffered(1))


def _mix_ffn_even_kernel(x_ref, c_ref, rf_ref, rb_ref, gd_ref, mf_ref, mb_ref, op_ref, g2_ref, ng_ref, w_ref,
                         n2_ref, mod_ref, w1_ref, w3_ref, w2_ref, o_ref, *, n_lat_tiles):
    m = mod_ref[0]
    g = _mm(_sigmoid(gd_ref[0]), g2_ref[...], NN, 2, 2)
    y1 = (rf_ref[0] + rb_ref[0]) * g
    y2 = _head_rms(mf_ref[0] + mb_ref[0], MLSTM_HEADS, MLSTM_HD) * ng_ref[...] * _sigmoid(op_ref[0])
    y = jnp.concatenate([y1, y2], axis=1).astype(BF16)
    h = _lat_or_ctx(x_ref, c_ref, n_lat_tiles) + m[2:3] * jnp.dot(y, w_ref[...], preferred_element_type=F32)
    o_ref[0] = _ffn_rows(h, m, n2_ref[...], w1_ref, w3_ref, w2_ref)


def _mix_ffn_even(x, ctx, rf, rb, proj, mf, mb, g2, ng, w_out, n2g, mod, w1, w3, w2, n_lat_tiles):
    b, seq, d = x.shape
    ff = w1.shape[1]
    nt = n_lat_tiles + 1
    tok = lambda width, col=0: pl.BlockSpec((1, TM, width), lambda i, t: (i, t, col))
    return pl.pallas_call(
        functools.partial(_mix_ffn_even_kernel, n_lat_tiles=n_lat_tiles),
        grid=(b, nt),
        in_specs=_lat_ctx_specs(d, n_lat_tiles) + [
            tok(RWKV_W), tok(RWKV_W), tok(128, EV_GD), tok(MLSTM_W), tok(MLSTM_W), tok(MLSTM_W, EV_O),
            _resident((GATE_LORA, RWKV_W)), _resident((1, MLSTM_W)), _resident((d, d)), _resident((1, d)),
            pl.BlockSpec((1, 6, d), lambda i, t: (jnp.where(t >= n_lat_tiles, 4, i), 0, 0)),
            _resident((d, ff)), _resident((d, ff)), _resident((ff, d)),
        ],
        out_specs=tok(d),
        out_shape=jax.ShapeDtypeStruct((b, nt * TM, d), F32),
        compiler_params=pltpu.CompilerParams(vmem_limit_bytes=VMEM_LIMIT),
        name="mix_ffn_even",
    )(x, ctx, rf, rb, proj, mf, mb, proj, g2, ng, w_out, n2g, mod, w1, w3, w2)


def _gelu_tanh(x):
    return 0.5 * x * (1.0 + jnp.tanh(0.7978845608028654 * (x + 0.044715 * x * x * x)))


def _gather_rows(ref):
    return jnp.concatenate([ref[0, :, r, :] for r in range(ref.shape[2])], axis=0)


def _mix_ffn_odd_kernel(h_ref, hf_ref, hb_ref, g_ref, lf_ref, lb_ref, gb_ref, ng_ref, w_ref, n2_ref, mod_ref,
                        w1_ref, w3_ref, w2_ref, fg_ref, o_ref):
    m = mod_ref[0]
    hsum = _gather_rows(hf_ref) + _gather_rows(hb_ref)
    y1 = _head_rms(hsum, HGRN_HEADS, HGRN_FD) * ng_ref[...] * _silu(_gather_rows(g_ref))
    y2 = (_gather_rows(lf_ref) + _gather_rows(lb_ref)) * _gelu_tanh(_gather_rows(gb_ref))
    y = jnp.concatenate([y1, y2], axis=1).astype(BF16)
    h = h_ref[0] + m[2:3] * jnp.dot(y, w_ref[...], preferred_element_type=F32)
    for lo in range(0, h.shape[0], TM):
        out = _ffn_rows(h[lo:lo + TM], m, n2_ref[...], w1_ref, w3_ref, w2_ref)
        ms = jnp.mean(out * out, axis=-1, keepdims=True)
        o_ref[0, lo:lo + TM, :] = out * lax.rsqrt(ms + EPS) * fg_ref[...]


def _mix_ffn_odd(h_all, hf, hb, proj, lf, lb, ng, w_out, n2g, mod, w1, w3, w2, final_g, rows):
    b, ltot, d = h_all.shape
    ff = w1.shape[1]
    tr = TM_ODD // GRID_W
    cm = lambda a: a.reshape(b, ltot // rows, rows, a.shape[-1])
    col = lambda width, c=0: pl.BlockSpec((1, GRID_W, tr, width), lambda i, t: (i, 0, t, c))
    hspec = pl.BlockSpec((1, TM_ODD, d), lambda i, t: (i, t, 0))
    return pl.pallas_call(
        _mix_ffn_odd_kernel,
        grid=(b, rows * GRID_W // TM_ODD),
        in_specs=[
            hspec, col(HGRN_W), col(HGRN_W), col(HGRN_W, OD_G), col(LRU_W), col(LRU_W), col(LRU_W, OD_GB),
            _resident((1, HGRN_W)), _resident((d, d)), _resident((1, d)),
            pl.BlockSpec((1, 6, d), lambda i, t: (i, 0, 0)),
            _resident((d, ff)), _resident((d, ff)), _resident((ff, d)), _resident((1, d)),
        ],
        out_specs=hspec,
        out_shape=jax.ShapeDtypeStruct((b, rows * GRID_W, d), F32),
        compiler_params=pltpu.CompilerParams(vmem_limit_bytes=VMEM_LIMIT),
        name="mix_ffn_odd",
    )(h_all, cm(hf), cm(hb), cm(proj), cm(lf), cm(lb), cm(proj), ng, w_out, n2g, mod, w1, w3, w2, final_g)


def _pack_even_in_w(w):
    r, k, v, wd, ad, gd, q, mk, mv, mo, ip, fp = jnp.split(
        w, [512, 1024, 1536, 1664, 1792, 1920, 2432, 2944, 3456, 3968, 3976], axis=1)
    lora = jnp.concatenate([wd[:, :LORA], ad[:, :LORA], wd[:, LORA:], ad[:, LORA:]], axis=1)
    gates = jnp.concatenate([ip, fp, jnp.zeros((w.shape[0], 128 - 16), w.dtype)], axis=1)
    return jnp.concatenate([r, k, v, q, mk, mv, mo, lora, gd, gates], axis=1).astype(BF16)


def _pack_rwkv_params(mu, w0, w2, a0, a2, kk, ka, rk, lnw, lnb):
    mu_x = mu[:, None, :3 * RWKV_W]
    mu_lo = mu[:, None, 3 * RWKV_W:]
    p512 = jnp.stack([w0, a0, kk, ka, lnw, lnb, rk.reshape(2, RWKV_W), jnp.zeros_like(w0)], axis=1)
    z = jnp.zeros_like(w2)
    wcomb = jnp.concatenate([jnp.concatenate([w2, z], axis=2), jnp.concatenate([z, a2], axis=2)], axis=1)
    return mu_x, mu_lo, p512, wcomb


def _pack_lru_params(wa, ba, wi, bi):
    def dense(w):
        eye = jnp.eye(LRU_BLOCKS, dtype=w.dtype)
        return jnp.einsum('dhij,hg->dhigj', w, eye).reshape(2, LRU_W, LRU_W)
    wg = jnp.concatenate([dense(wa), dense(wi)], axis=2)
    bias = jnp.concatenate([ba, bi], axis=1)[:, None, :]
    return wg, bias


def kernel(x, c, ctx, c_ctx, norm1_g, norm2_g, mod_w, mod_b, ffn_w1, ffn_w3, ffn_w2, final_g, ev_in_w, ev_out_w, rwkv_mu, rwkv_w0, rwkv_w2, rwkv_a0, rwkv_a2, rwkv_kk, rwkv_ka, rwkv_rk, rwkv_lnw, rwkv_lnb, rwkv_g2, mlstm_bi, mlstm_bf, mlstm_ng, od_in_w, od_out_w, hgrn_lb, hgrn_ng, lru_conv_w, lru_conv_b, lru_wa, lru_ba, lru_wi, lru_bi, lru_lam):
    b, seq, d = x.shape
    n_ctx_tok = ctx.shape[1]
    rows = seq // GRID_W
    n_lat_tiles = seq // TM
    depth = mod_w.shape[0]

    assert depth == 2, "layer 0 reads x / ctx directly and the odd layer is the last one"
    c8 = jnp.concatenate([c, c_ctx[None, :], jnp.zeros((8 - b - 1, d), F32)], axis=0)
    mod = _modulation(c8, mod_w, mod_b).reshape(depth, 8, 6, d)

    h_all = None
    for l in range(depth):
        j = l // 2
        g1 = norm1_g[l][None, :]
        ffn_w = (ffn_w1[l].astype(BF16), ffn_w3[l].astype(BF16), ffn_w2[l].astype(BF16))
        if l % 2 == 0:
            proj = _in_proj_even(x, ctx, g1, mod[l], _pack_even_in_w(ev_in_w[j]), n_lat_tiles)
            mu_x, mu_lo, p512, wcomb = _pack_rwkv_params(
                rwkv_mu[j], rwkv_w0[j], rwkv_w2[j], rwkv_a0[j], rwkv_a2[j], rwkv_kk[j], rwkv_ka[j],
                rwkv_rk[j], rwkv_lnw[j], rwkv_lnb[j])
            rf, rb = _rwkv(proj, mu_x, mu_lo, p512, wcomb, seq // T_RWKV, n_ctx_tok // T_RWKV, bb=2)
            gbias = jnp.concatenate([mlstm_bi[j].reshape(-1), mlstm_bf[j].reshape(-1),
                                     jnp.zeros((128 - 4 * MLSTM_HEADS,), F32)])[None, :]
            mf, mb = _mlstm(proj, gbias, seq // T_MLSTM, n_ctx_tok // T_MLSTM, bb=2)
            h_all = _mix_ffn_even(x, ctx, rf, rb, proj, mf, mb, rwkv_g2[j], mlstm_ng[j][None, :],
                                  ev_out_w[j].astype(BF16), norm2_g[l][None, :], mod[l], *ffn_w, n_lat_tiles)
        else:
            proj = _in_proj_odd(h_all, g1, mod[l], od_in_w[j].astype(BF16), rows)
            hf, hb = _hgrn(proj, hgrn_lb, l, seq // T_HGRN, n_ctx_tok // T_HGRN, bb=2)
            wg, bias = _pack_lru_params(lru_wa[j], lru_ba[j], lru_wi[j], lru_bi[j])
            lf, lb = _lru(proj, lru_conv_w[j], lru_conv_b[j][None, :], wg, bias, lru_lam[j][:, None, :],
                          seq // T_LRU, n_ctx_tok // T_LRU)
            h_all = _mix_ffn_odd(h_all, hf, hb, proj, lf, lb, hgrn_ng[j][None, :], od_out_w[j].astype(BF16),
                                 norm2_g[l][None, :], mod[l], *ffn_w, final_g[None, :], rows)
    return h_all
```

```python
import functools

import jax
import jax.numpy as jnp
from jax import lax
from jax.experimental import pallas as pl
from jax.experimental.pallas import tpu as pltpu

F32 = jnp.float32
BF16 = jnp.bfloat16

D_MODEL = 1024
GRID_W = 64
D_FF = 2816
EPS = 1e-6

RWKV_HEADS = 8
RWKV_HD = 64
RWKV_W = RWKV_HEADS * RWKV_HD
LORA = 64
GATE_LORA = 128
RWKV_GN_EPS = 64e-5
MLSTM_HEADS = 4
MLSTM_HD = 128
MLSTM_W = MLSTM_HEADS * MLSTM_HD
HGRN_HEADS = 4
HGRN_FD = 128
HGRN_W = HGRN_HEADS * HGRN_FD
LRU_W = 512
LRU_BLOCKS = 8
LRU_BD = LRU_W // LRU_BLOCKS
LRU_C = 8.0
SUBLANES = 8

T_RWKV = 64
T_MLSTM = 128
T_HGRN = 64
T_LRU = 256
TM = 256
TM_ODD = 8 * GRID_W

EV_N = 4096
EV_RKV, EV_Q, EV_K, EV_V, EV_O = 0, 3, 4, 5, 6
EV_LORA, EV_GD, EV_GATES = 28, 30, 31
OD_N = 3584
OD_Q, OD_F, OD_I, OD_G, OD_XB, OD_GB = 0, 1, 3, 4, 5, 6

VMEM_LIMIT = 56 * 1024 * 1024

NT = ((1,), (1,))
NN = ((1,), (0,))
TN = ((0,), (0,))


def _split(x, n):
    parts = []
    r = x
    for i in range(n):
        p = r.astype(BF16)
        parts.append(p)
        if i + 1 < n:
            r = r - p.astype(F32)
    return parts


def _mm(a, b, dims=NN, pa=1, pb=1):
    dn = (dims, ((), ()))
    ap = _split(a, pa) if a.dtype != BF16 else [a]
    bp = _split(b, pb) if b.dtype != BF16 else [b]
    n = max(len(ap), len(bp))
    acc = None
    for i, x in enumerate(ap):
        for j, y in enumerate(bp):
            if i + j < n:
                t = lax.dot_general(x, y, dn, preferred_element_type=F32)
                acc = t if acc is None else acc + t
    return acc


def _sigmoid(x):
    return 1.0 / (1.0 + jnp.exp(-x))


def _silu(x):
    return x * _sigmoid(x)


def _log_sigmoid(x):
    return jnp.minimum(x, 0.0) - jnp.log1p(jnp.exp(-jnp.abs(x)))


def _softplus(x):
    return jnp.maximum(x, 0.0) + jnp.log1p(jnp.exp(-jnp.abs(x)))


def _masks(T, reverse):
    ti = lax.broadcasted_iota(jnp.int32, (T, T), 0)
    si = lax.broadcasted_iota(jnp.int32, (T, T), 1)
    if reverse:
        return si >= ti, si > ti
    return si <= ti, si < ti


def _ones_mask(incl):
    return jnp.where(incl, 1.0, 0.0).astype(BF16)


def _shift_prev(x, carry_row, reverse):
    T = x.shape[0]
    row = lax.broadcasted_iota(jnp.int32, x.shape, 0)
    if reverse:
        return jnp.where(row == T - 1, carry_row, pltpu.roll(x, T - 1, 0))
    return jnp.where(row == 0, carry_row, pltpu.roll(x, 1, 0))


def _fwd_blk(i, n_lat, n_ctx):
    return jnp.where(i < n_ctx, n_lat + i, i - n_ctx)


def _bwd_blk(i, n_lat, n_ctx):
    return n_lat + n_ctx - 1 - i


def _mod_kernel(c_ref, w_ref, b_ref, o_ref):
    c = c_ref[...]
    o_ref[0] = _mm(_silu(c), w_ref[0], NN, 3, 3) + b_ref[0]


def _modulation(c8, mod_w, mod_b):
    depth, d, n = mod_w.shape
    return pl.pallas_call(
        _mod_kernel,
        grid=(depth, n // d),
        in_specs=[
            pl.BlockSpec((8, d), lambda l, j: (0, 0)),
            pl.BlockSpec((1, d, d), lambda l, j: (l, 0, j)),
            pl.BlockSpec((1, 1, d), lambda l, j: (l, 0, j)),
        ],
        out_specs=pl.BlockSpec((1, 8, d), lambda l, j: (l, 0, j)),
        out_shape=jax.ShapeDtypeStruct((depth, 8, n), F32),
        compiler_params=pltpu.CompilerParams(vmem_limit_bytes=VMEM_LIMIT),
        name="adaln_modulation",
    )(c8, mod_w, mod_b.reshape(depth, 1, n))


def _norm_mod(x, g, shift, scale):
    ms = jnp.mean(x * x, axis=-1, keepdims=True)
    return (x * lax.rsqrt(ms + EPS) * g) * (1.0 + scale) + shift


def _lat_or_ctx(x_ref, c_ref, n_lat_tiles):
    return jnp.where(pl.program_id(1) < n_lat_tiles, x_ref[0], c_ref[0])


def _lat_ctx_specs(d, n_lat_tiles):
    return [
        pl.BlockSpec((1, TM, d), lambda i, t: (i, jnp.minimum(t, n_lat_tiles - 1), 0)),
        pl.BlockSpec((1, TM, d), lambda i, t: (i, 0, 0)),
    ]


def _in_even_kernel(x_ref, c_ref, g_ref, mod_ref, w_ref, o_ref, *, n_lat_tiles):
    m = mod_ref[0]
    u = _norm_mod(_lat_or_ctx(x_ref, c_ref, n_lat_tiles), g_ref[...], m[0:1], m[1:2])
    o_ref[0] = jnp.dot(u.astype(BF16), w_ref[...], preferred_element_type=F32)


def _in_proj_even(x, ctx, g, mod, w, n_lat_tiles):
    b, seq, d = x.shape
    assert ctx.shape[1] == TM
    n = w.shape[1]
    nt = n_lat_tiles + 1
    return pl.pallas_call(
        functools.partial(_in_even_kernel, n_lat_tiles=n_lat_tiles),
        grid=(b, nt),
        in_specs=_lat_ctx_specs(d, n_lat_tiles) + [
            pl.BlockSpec((1, d), lambda i, t: (0, 0)),
            pl.BlockSpec((1, 6, d), lambda i, t: (jnp.where(t >= n_lat_tiles, 4, i), 0, 0)),
            pl.BlockSpec((d, n), lambda i, t: (0, 0)),
        ],
        out_specs=pl.BlockSpec((1, TM, n), lambda i, t: (i, t, 0)),
        out_shape=jax.ShapeDtypeStruct((b, nt * TM, n), F32),
        compiler_params=pltpu.CompilerParams(vmem_limit_bytes=VMEM_LIMIT),
        name="in_proj_even",
    )(x, ctx, g, mod, w)


def _in_odd_kernel(x_ref, g_ref, mod_ref, w_ref, *rest, scatter):
    o_ref = rest[-1]
    m = mod_ref[0]
    if scatter:
        x = jnp.concatenate([x_ref[0, :, w, :] for w in range(GRID_W)], axis=0)
    else:
        x = x_ref[0]
    u = _norm_mod(x, g_ref[...], m[0:1], m[1:2])
    y = jnp.dot(u.astype(BF16), w_ref[...], preferred_element_type=F32)
    if scatter:
        o_ref[0] = y.reshape(o_ref.shape[1:])
    else:
        for j in range(o_ref.shape[1]):
            o_ref[0, j] = y[j * o_ref.shape[2]:(j + 1) * o_ref.shape[2], :]


def _in_proj_odd(h_all, g, mod, w, rows):
    b, ltot, d = h_all.shape
    n = w.shape[1]
    lat = rows * GRID_W
    n_ctx_slots = (ltot - lat) // rows
    tr = TM_ODD // GRID_W
    out_shape = jax.ShapeDtypeStruct((b, GRID_W + n_ctx_slots, rows, n), F32)
    common = [
        pl.BlockSpec((1, d), lambda i, t: (0, 0)),
        None,
        pl.BlockSpec((d, n), lambda i, t: (0, 0)),
    ]
    lat_specs = list(common)
    lat_specs[1] = pl.BlockSpec((1, 6, d), lambda i, t: (i, 0, 0))
    proj = pl.pallas_call(
        functools.partial(_in_odd_kernel, scatter=True),
        grid=(b, lat // TM_ODD),
        in_specs=[pl.BlockSpec((1, tr, GRID_W, d), lambda i, t: (i, t, 0, 0))] + lat_specs,
        out_specs=pl.BlockSpec((1, GRID_W, tr, n), lambda i, t: (i, 0, t, 0)),
        out_shape=out_shape,
        compiler_params=pltpu.CompilerParams(vmem_limit_bytes=VMEM_LIMIT),
        name="in_proj_odd",
    )(h_all.reshape(b, ltot // GRID_W, GRID_W, d), g, mod, w)
    ctx_tok = ltot - lat
    ctx_specs = list(common)
    ctx_specs[1] = pl.BlockSpec((1, 6, d), lambda i, t: (4, 0, 0))
    return pl.pallas_call(
        functools.partial(_in_odd_kernel, scatter=False),
        grid=(b, 1),
        in_specs=[pl.BlockSpec((1, ctx_tok, d), lambda i, t: (i, lat // ctx_tok, 0))] + ctx_specs
        + [pl.BlockSpec(memory_space=pl.ANY)],
        out_specs=pl.BlockSpec((1, n_ctx_slots, rows, n), lambda i, t: (i, GRID_W // n_ctx_slots, 0, 0)),
        out_shape=out_shape,
        input_output_aliases={4: 0},
        compiler_params=pltpu.CompilerParams(vmem_limit_bytes=VMEM_LIMIT),
        name="in_proj_odd_ctx",
    )(h_all, g, mod, w, proj).reshape(b, ltot, n)


def _head_sums(parts, ones_blk):
    T = parts[0].shape[0]
    s = _mm(jnp.concatenate(parts, axis=0), ones_blk, NN, 1, 1)
    return [s[i * T:(i + 1) * T] for i in range(len(parts))]


def _rwkv_shift(x, lo, carry_x, carry_lo, mu_x, mu_lo, reverse):
    xs = x + mu_x * (_shift_prev(x, carry_x, reverse) - x)
    los = lo + mu_lo * (_shift_prev(lo, carry_lo, reverse) - lo)
    lane = lax.broadcasted_iota(jnp.int32, los.shape, 1)
    return xs, jnp.where(lane < LORA, jnp.tanh(los), los)


def _rwkv_operands(xs, wa, ss, p512, reverse):
    T = xs.shape[0]
    incl, _ = _masks(T, reverse)
    edge = 0 if reverse else T - 1
    r = xs[:, :RWKV_W]
    k = xs[:, RWKV_W:2 * RWKV_W]
    v = xs[:, 2 * RWKV_W:]
    w0, a0, k_k, k_a = (p512[i:i + 1, :] for i in range(4))
    w = -_softplus(-(w0 + wa[:, :RWKV_W])) - 0.5
    a = _sigmoid(a0 + wa[:, RWKV_W:])
    logdec = -jnp.exp(w)
    kk = (k * k_k) / jnp.maximum(jnp.sqrt(ss), 1e-12)
    k2 = k * (1.0 + (a - 1.0) * k_a)
    cl = _mm(_ones_mask(incl), logdec, NN, 1, 3)
    e_pos = jnp.exp(cl)
    e_neg = jnp.exp(-cl)
    return dict(
        A=kk * jnp.exp(cl - logdec), B=kk * a * e_neg, K=k2 * e_neg, R=r * e_pos, V=v,
        gT=e_pos[edge:edge + 1, :], r=r, k2=k2)


def _rwkv_chunks(units, states, T, p):
    mm = functools.partial(_mm, pa=p, pb=p)
    pk = (lambda t: t.astype(BF16)) if p == 1 else (lambda t: t)
    hd = RWKV_HD
    pw = 2 * hd

    def bd(x):
        lo = lax.broadcasted_iota(jnp.int32, x.shape, 1) < hd
        return pk(jnp.concatenate([jnp.where(lo, x, 0.0), jnp.where(lo, 0.0, x)], axis=0))

    jobs = []
    for ui, (u, reverse) in enumerate(units):
        for pi in range(RWKV_HEADS // 2):
            sl = slice(pi * pw, (pi + 1) * pw)
            jobs.append(dict(
                ui=ui, reverse=reverse,
                AR=pk(jnp.concatenate([u['A'][:, sl], u['R'][:, sl]], axis=0)),
                B=u['B'][:, sl], K=u['K'][:, sl], V=u['V'][:, sl], gT=u['gT'][:, sl],
                S=states[ui][pi]))
    ti = lax.broadcasted_iota(jnp.int32, (2 * T, pw), 0)
    si = lax.broadcasted_iota(jnp.int32, (2 * T, pw), 1) & (hd - 1)
    t1 = lax.broadcasted_iota(jnp.int32, (T, pw), 0)
    s1 = lax.broadcasted_iota(jnp.int32, (T, pw), 1) & (hd - 1)
    masks = {}
    for reverse in (False, True):
        if reverse:
            masks[reverse] = (si > jnp.where(ti < T, ti, ti - T - 1), s1 > t1, s1 >= t1)
        else:
            masks[reverse] = (si < jnp.where(ti < T, ti, ti - T + 1), s1 < t1, s1 <= t1)
    for j in jobs:
        j['GB'] = mm(j['AR'], bd(j['B']), NT)
    for j in jobs:
        j['GK'] = mm(j['AR'], bd(j['K']), NT)
    for j in jobs:
        j['SS'] = mm(j['AR'], bd(j['S']), NT)
    for j in jobs:
        both, strict, _ = masks[j['reverse']]
        j['LV'] = mm(jnp.where(both, j['GK'], 0.0), bd(j['V']))
        j['N'] = jnp.where(strict, j['GB'][:T], 0.0)
    rc = t1 ^ s1
    eye = jnp.where(rc == 0, 1.0, 0.0)
    for j in jobs:
        j['D'] = eye - jnp.where(rc == 1, j['N'], 0.0)
    s = 2
    while s < T:
        level = (rc >= s) & (rc < 2 * s)
        for j in jobs:
            j['LD'] = mm(jnp.where(level, j['N'], 0.0), bd(j['D']))
        for j in jobs:
            j['D'] = j['D'] - mm(j['D'], bd(j['LD']))
        s *= 2
    for j in jobs:
        j['X'] = mm(j['D'], bd(j['SS'][:T] + j['LV'][:T]))
    for j in jobs:
        incl = masks[j['reverse']][2]
        j['O'] = j['SS'][T:] + j['LV'][T:] - mm(jnp.where(incl, j['GB'][T:], 0.0), bd(j['X']))
    lo = lax.broadcasted_iota(jnp.int32, (hd, pw), 1) < hd
    for j in jobs:
        full = mm(jnp.concatenate([j['V'], -j['X']], axis=0), jnp.concatenate([j['K'], j['B']], axis=0), TN)
        j['S_new'] = (j['S'] + jnp.where(lo, full[:hd], full[hd:])) * j['gT']
    outs, new_states = [], []
    for ui in range(len(units)):
        mine = [j for j in jobs if j['ui'] == ui]
        outs.append(jnp.concatenate([j['O'] for j in mine], axis=1))
        new_states.append([j['S_new'] for j in mine])
    return outs, new_states


def _rwkv_kernel(xf_ref, xb_ref, lf_ref, lb_ref, mux_ref, mulo_ref, p512_ref, wcomb_ref, ones_ref,
                 of_ref, ob_ref, s_ref, cx_ref, cl_ref, *, n_ctx, prec):
    i = pl.program_id(1)
    T = xf_ref.shape[1]

    @pl.when(i == 0)
    def _():
        s_ref[...] = jnp.zeros_like(s_ref)

    @pl.when((i == 0) | (i == n_ctx))
    def _():
        cx_ref[...] = jnp.zeros_like(cx_ref)
        cl_ref[...] = jnp.zeros_like(cl_ref)

    ones_blk = ones_ref[...]
    refs = ((xf_ref, lf_ref, of_ref), (xb_ref, lb_ref, ob_ref))
    nb = xf_ref.shape[0]
    where = [(bi, d) for d in range(2) for bi in range(nb)]
    xs, zs = [], []
    for bi, d in where:
        x = refs[d][0][bi]
        lo = refs[d][1][bi]
        edge = 0 if d else T - 1
        xs_u, z_u = _rwkv_shift(x, lo, cx_ref[bi, d, 0:1, :], cl_ref[bi, d, 0:1, :], mux_ref[d],
                                mulo_ref[d], bool(d))
        cx_ref[bi, d, 0:1, :] = x[edge:edge + 1, :]
        cl_ref[bi, d, 0:1, :] = lo[edge:edge + 1, :]
        xs.append(xs_u)
        zs.append(z_u)
    was = []
    for d in range(2):
        wa = _mm(jnp.concatenate(zs[d * nb:(d + 1) * nb], axis=0), wcomb_ref[d], NN, 2, 2)
        was += [wa[bi * T:(bi + 1) * T] for bi in range(nb)]
    kk0 = [x_u[:, RWKV_W:2 * RWKV_W] * p512_ref[d][2:3, :] for x_u, (_, d) in zip(xs, where)]
    sss = _head_sums([k0 * k0 for k0 in kk0], ones_blk)
    units = [(_rwkv_operands(x_u, wa, ss, p512_ref[d], bool(d)), bool(d))
             for x_u, wa, ss, (_, d) in zip(xs, was, sss, where)]
    states = [[s_ref[bi, d, h] for h in range(RWKV_HEADS // 2)] for bi, d in where]
    outs, new_states = _rwkv_chunks(units, states, T, prec)
    for (bi, d), ns in zip(where, new_states):
        for h in range(RWKV_HEADS // 2):
            s_ref[bi, d, h] = ns[h]
    inv_n = 1.0 / RWKV_HD
    means = _head_sums(outs, ones_blk)
    ocs = [o - mean * inv_n for o, mean in zip(outs, means)]
    varis = _head_sums([oc * oc for oc in ocs], ones_blk)
    rks = _head_sums([u['r'] * u['k2'] * p512_ref[d][6:7, :] for (u, _), (_, d) in zip(units, where)], ones_blk)
    for (bi, d), (u, _), oc, var, rk in zip(where, units, ocs, varis, rks):
        ln_w, ln_b = p512_ref[d][4:5, :], p512_ref[d][5:6, :]
        gn = oc * lax.rsqrt(var * inv_n + RWKV_GN_EPS) * ln_w + ln_b
        refs[d][2][bi] = gn + rk * u['V']


def _rwkv(proj, mu_x, mu_lo, p512, wcomb, n_lat, n_ctx, prec=1, bb=1):
    b, ltot, _ = proj.shape
    T = T_RWKV
    ones_blk = jnp.kron(jnp.eye(RWKV_HEADS, dtype=F32), jnp.ones((RWKV_HD, RWKV_HD), F32)).astype(BF16)
    fwd = functools.partial(_fwd_blk, n_lat=n_lat, n_ctx=n_ctx)
    bwd = functools.partial(_bwd_blk, n_lat=n_lat, n_ctx=n_ctx)
    full = lambda shape: pl.BlockSpec(shape, lambda bi, i: (0,) * len(shape))
    out = jax.ShapeDtypeStruct((b, ltot, RWKV_W), F32)
    return pl.pallas_call(
        functools.partial(_rwkv_kernel, n_ctx=n_ctx, prec=prec),
        grid=(b // bb, n_lat + n_ctx),
        in_specs=[
            pl.BlockSpec((bb, T, 3 * RWKV_W), lambda bi, i: (bi, fwd(i), EV_RKV)),
            pl.BlockSpec((bb, T, 3 * RWKV_W), lambda bi, i: (bi, bwd(i), EV_RKV)),
            pl.BlockSpec((bb, T, 128), lambda bi, i: (bi, fwd(i), EV_LORA)),
            pl.BlockSpec((bb, T, 128), lambda bi, i: (bi, bwd(i), EV_LORA + 1)),
            full((2, 1, 3 * RWKV_W)), full((2, 1, 128)), full((2, 8, RWKV_W)),
            full((2, 128, 2 * RWKV_W)), full((RWKV_W, RWKV_W)),
        ],
        out_specs=[
            pl.BlockSpec((bb, T, RWKV_W), lambda bi, i: (bi, fwd(i), 0)),
            pl.BlockSpec((bb, T, RWKV_W), lambda bi, i: (bi, bwd(i), 0)),
        ],
        out_shape=[out, out],
        scratch_shapes=[
            pltpu.VMEM((bb, 2, RWKV_HEADS // 2, RWKV_HD, 2 * RWKV_HD), F32),
            pltpu.VMEM((bb, 2, 8, 3 * RWKV_W), F32),
            pltpu.VMEM((bb, 2, 8, 128), F32),
        ],
        compiler_params=pltpu.CompilerParams(
            dimension_semantics=("arbitrary", "arbitrary"), vmem_limit_bytes=VMEM_LIMIT),
        name="rwkv7_chunked",
    )(proj, proj, proj, proj, mu_x, mu_lo, p512, wcomb, ones_blk)


def _mlstm_chunks(units, gbias, c_ref, n_ref, m_ref, p):
    mm = functools.partial(_mm, pa=p, pb=p)
    pk = (lambda t: t.astype(BF16)) if p == 1 else (lambda t: t)
    jobs = []
    for base, d, q, k, v, gates, reverse in units:
        T = q.shape[0]
        incl, _ = _masks(T, reverse)
        edge = 0 if reverse else T - 1
        g = gates + gbias
        lane = lax.broadcasted_iota(jnp.int32, g.shape, 1)
        g = jnp.where(lane < 2 * MLSTM_HEADS, g, _log_sigmoid(g))
        gt = g.T
        ones = _ones_mask(incl)
        b_col = _mm(ones, g, NN, 1, 3)
        b_row = _mm(gt, ones, NT, 3, 1)
        for h in range(MLSTM_HEADS):
            sl = slice(h * MLSTM_HD, (h + 1) * MLSTM_HD)
            ci = d * MLSTM_HEADS + h
            cf = 2 * MLSTM_HEADS + ci
            st = base + h
            jobs.append(dict(
                st=st, incl=incl, edge=edge,
                q=pk(q[:, sl] * (MLSTM_HD ** -0.5)), k=k[:, sl], kp=pk(k[:, sl]), v=v[:, sl],
                bc=b_col[:, cf:cf + 1], br=b_row[cf:cf + 1, :], li_c=g[:, ci:ci + 1], li_r=gt[ci:ci + 1, :],
                C0=c_ref[st], n0=n_ref[st, 0:1, :], m0=m_ref[st, 0:1, 0:1]))
    for j in jobs:
        j['qk'] = mm(j['q'], j['kp'], NT)
    for j in jobs:
        j['qC'] = mm(j['q'], j['C0'], NT)
    for j in jobs:
        logD = jnp.where(j['incl'], j['bc'] - j['br'] + j['li_r'], -jnp.inf)
        m_prev = j['bc'] + j['m0']
        j['m'] = jnp.maximum(m_prev, jnp.max(logD, axis=-1, keepdims=True))
        j['s'] = pk(j['qk'] * jnp.exp(logD - j['m']))
        j['inter'] = jnp.exp(m_prev - j['m'])
        bT = j['bc'][j['edge']:j['edge'] + 1, :]
        gsum = bT - j['bc'] + j['li_c']
        j['m_new'] = jnp.maximum(bT + j['m0'], jnp.max(gsum, axis=0, keepdims=True))
        j['wts'] = jnp.exp(gsum - j['m_new'])
        j['cd'] = jnp.exp(bT + j['m0'] - j['m_new'])
    for j in jobs:
        j['sv'] = mm(j['s'], j['v'])
    ones_tt = jnp.ones((jobs[0]['s'].shape[1], MLSTM_HD), BF16)
    for j in jobs:
        j['rs'] = mm(j['s'], ones_tt)
    for j in jobs:
        j['qn'] = mm(j['q'], jnp.broadcast_to(j['n0'], (MLSTM_HD, MLSTM_HD)), NT)
    for j in jobs:
        j['dC'] = mm(j['wts'] * j['v'], j['kp'], TN)
    outs = []
    for j in jobs:
        num = j['sv'] + j['inter'] * j['qC']
        den = j['rs'] + j['inter'] * j['qn']
        outs.append(num / jnp.maximum(jnp.abs(den), jnp.exp(-j['m'])))
        st = j['st']
        c_ref[st] = j['cd'] * j['C0'] + j['dC']
        n_ref[st, 0:1, :] = j['cd'] * j['n0'] + jnp.sum(j['wts'] * j['k'], axis=0, keepdims=True)
        m_ref[st] = jnp.broadcast_to(j['m_new'], m_ref.shape[1:])
    return [jnp.concatenate(outs[u * MLSTM_HEADS:(u + 1) * MLSTM_HEADS], axis=1) for u in range(len(units))]


def _mlstm_kernel(qf_ref, kf_ref, vf_ref, gf_ref, qb_ref, kb_ref, vb_ref, gb_ref, gbias_ref,
                  of_ref, ob_ref, c_ref, n_ref, m_ref, *, prec):
    i = pl.program_id(1)

    @pl.when(i == 0)
    def _():
        c_ref[...] = jnp.zeros_like(c_ref)
        n_ref[...] = jnp.zeros_like(n_ref)
        m_ref[...] = jnp.zeros_like(m_ref)

    refs = ((qf_ref, kf_ref, vf_ref, gf_ref, of_ref), (qb_ref, kb_ref, vb_ref, gb_ref, ob_ref))
    units, where = [], []
    for bi in range(qf_ref.shape[0]):
        for d, (q_ref, k_ref, v_ref, g_ref, o_ref) in enumerate(refs):
            units.append(((bi * 2 + d) * MLSTM_HEADS, d, q_ref[bi], k_ref[bi], v_ref[bi], g_ref[bi], bool(d)))
            where.append((o_ref, bi))
    outs = _mlstm_chunks(units, gbias_ref[...], c_ref, n_ref, m_ref, prec)
    for (o_ref, bi), o in zip(where, outs):
        o_ref[bi] = o


def _mlstm(proj, gbias, n_lat, n_ctx, prec=1, bb=1):
    b, ltot, _ = proj.shape
    T = T_MLSTM
    fwd = functools.partial(_fwd_blk, n_lat=n_lat, n_ctx=n_ctx)
    bwd = functools.partial(_bwd_blk, n_lat=n_lat, n_ctx=n_ctx)
    spec = lambda width, col, order: pl.BlockSpec((bb, T, width), lambda bi, i: (bi, order(i), col))
    out = jax.ShapeDtypeStruct((b, ltot, MLSTM_W), F32)
    nst = bb * 2 * MLSTM_HEADS
    return pl.pallas_call(
        functools.partial(_mlstm_kernel, prec=prec),
        grid=(b // bb, n_lat + n_ctx),
        in_specs=[
            spec(MLSTM_W, EV_Q, fwd), spec(MLSTM_W, EV_K, fwd), spec(MLSTM_W, EV_V, fwd),
            spec(128, EV_GATES, fwd),
            spec(MLSTM_W, EV_Q, bwd), spec(MLSTM_W, EV_K, bwd), spec(MLSTM_W, EV_V, bwd),
            spec(128, EV_GATES, bwd),
            pl.BlockSpec((1, 128), lambda bi, i: (0, 0)),
        ],
        out_specs=[
            pl.BlockSpec((bb, T, MLSTM_W), lambda bi, i: (bi, fwd(i), 0)),
            pl.BlockSpec((bb, T, MLSTM_W), lambda bi, i: (bi, bwd(i), 0)),
        ],
        out_shape=[out, out],
        scratch_shapes=[
            pltpu.VMEM((nst, MLSTM_HD, MLSTM_HD), F32),
            pltpu.VMEM((nst, 8, MLSTM_HD), F32),
            pltpu.VMEM((nst, 8, 128), F32),
        ],
        compiler_params=pltpu.CompilerParams(
            dimension_semantics=("arbitrary", "arbitrary"), vmem_limit_bytes=VMEM_LIMIT),
        name="mlstm_chunked",
    )(proj, proj, proj, proj, proj, proj, proj, proj, gbias)


def _hgrn_chunks(units, lb, s_ref, p):
    mm = functools.partial(_mm, pa=p, pb=p)
    jobs = []
    for base, q, fpre, vi, reverse in units:
        T = q.shape[0]
        incl, _ = _masks(T, reverse)
        edge = 0 if reverse else T - 1
        f = lb + (1.0 - lb) * _sigmoid(fpre)
        kk = 1.0 - f
        qs = _silu(q)
        bc = _mm(_ones_mask(incl), jnp.log(f), NN, 1, 3)
        bmid = bc[T // 2:T // 2 + 1, :]
        bT = bc[edge:edge + 1, :]
        qt = qs * jnp.exp(bc - bmid)
        kt = kk * jnp.exp(bmid - bc)
        qd = qs * jnp.exp(bc)
        kd = kk * jnp.exp(bT - bc)
        eT = jnp.exp(bT)
        for h in range(HGRN_HEADS):
            sl = slice(h * HGRN_FD, (h + 1) * HGRN_FD)
            jobs.append(dict(st=base + h, incl=incl, qt=qt[:, sl], kt=kt[:, sl], qd=qd[:, sl], kd=kd[:, sl],
                             v=vi[:, sl], eT=eT[:, sl], S=s_ref[base + h]))
    for j in jobs:
        j['A'] = jnp.where(j['incl'], mm(j['qt'], j['kt'], NT), 0.0)
    for j in jobs:
        j['qS'] = mm(j['qd'], j['S'], NT)
    for j in jobs:
        j['o'] = mm(j['A'], j['v']) + j['qS']
    for j in jobs:
        s_ref[j['st']] = j['S'] * j['eT'] + mm(j['v'], j['kd'], TN)
    outs = [j['o'] for j in jobs]
    return [jnp.concatenate(outs[u * HGRN_HEADS:(u + 1) * HGRN_HEADS], axis=1) for u in range(len(units))]


def _hgrn_kernel(qf_ref, ff_ref, if_ref, qb_ref, fb_ref, ib_ref, lb_ref, of_ref, ob_ref, s_ref,
                 *, layer, prec):
    i = pl.program_id(1)

    @pl.when(i == 0)
    def _():
        s_ref[...] = jnp.zeros_like(s_ref)

    raw = lb_ref[...]
    e = jnp.exp(raw - jnp.max(raw, axis=0, keepdims=True))
    sm = e / jnp.sum(e, axis=0, keepdims=True)
    lb = jnp.sum(sm[:layer + 1], axis=0, keepdims=True) - sm[0:1]
    refs = ((qf_ref, ff_ref, if_ref, of_ref), (qb_ref, fb_ref, ib_ref, ob_ref))
    units, where = [], []
    for bi in range(qf_ref.shape[0]):
        for d, (q_ref, f_ref, v_ref, o_ref) in enumerate(refs):
            units.append(((bi * 2 + d) * HGRN_HEADS, q_ref[bi], f_ref[bi], v_ref[bi], bool(d)))
            where.append((o_ref, bi))
    outs = _hgrn_chunks(units, lb, s_ref, prec)
    for (o_ref, bi), o in zip(where, outs):
        o_ref[bi] = o


def _hgrn(proj, hgrn_lb, layer, n_lat, n_ctx, prec=1, bb=1):
    b, ltot, _ = proj.shape
    T = T_HGRN
    fwd = functools.partial(_fwd_blk, n_lat=n_lat, n_ctx=n_ctx)
    bwd = functools.partial(_bwd_blk, n_lat=n_lat, n_ctx=n_ctx)
    spec = lambda col, order: pl.BlockSpec((bb, T, HGRN_W), lambda bi, i: (bi, order(i), col))
    out = jax.ShapeDtypeStruct((b, ltot, HGRN_W), F32)
    return pl.pallas_call(
        functools.partial(_hgrn_kernel, layer=layer, prec=prec),
        grid=(b // bb, n_lat + n_ctx),
        in_specs=[
            spec(OD_Q, fwd), spec(OD_F, fwd), spec(OD_I, fwd),
            spec(OD_Q, bwd), spec(OD_F + 1, bwd), spec(OD_I, bwd),
            pl.BlockSpec(hgrn_lb.shape, lambda bi, i: (0, 0)),
        ],
        out_specs=[
            pl.BlockSpec((bb, T, HGRN_W), lambda bi, i: (bi, fwd(i), 0)),
            pl.BlockSpec((bb, T, HGRN_W), lambda bi, i: (bi, bwd(i), 0)),
        ],
        out_shape=[out, out],
        scratch_shapes=[pltpu.VMEM((bb * 2 * HGRN_HEADS, HGRN_FD, HGRN_FD), F32)],
        compiler_params=pltpu.CompilerParams(
            dimension_semantics=("arbitrary", "arbitrary"), vmem_limit_bytes=VMEM_LIMIT),
        name="hgrn2_chunked",
    )(proj, proj, proj, proj, proj, proj, hgrn_lb)


def _lru_conv(x, halo_prev, halo_next, has_prev, has_next, cw, cb):
    T = x.shape[0]
    row = lax.broadcasted_iota(jnp.int32, x.shape, 0)
    hp = jnp.where(has_prev, halo_prev, 0.0)
    hn = jnp.where(has_next, halo_next, 0.0)
    x_m1 = jnp.where(row == 0, hp[7:8, :], pltpu.roll(x, 1, 0))
    x_m2 = jnp.where(row == 0, hp[6:7, :], jnp.where(row == 1, hp[7:8, :], pltpu.roll(x, 2, 0)))
    x_p1 = jnp.where(row == T - 1, hn[0:1, :], pltpu.roll(x, T - 1, 0))
    return x_m2 * cw[0:1, :] + x_m1 * cw[1:2, :] + x * cw[2:3, :] + x_p1 * cw[3:4, :] + cb


def _lru_dir(d, xc, wg, bias, sp_lam, h_ref, reverse, p):
    T = xc.shape[0]
    gates = _sigmoid(_mm(xc, wg, NN, p, p) + bias)
    r = gates[:, :LRU_W]
    ig = gates[:, LRU_W:]
    log_a = -LRU_C * r * sp_lam
    a = jnp.exp(log_a)
    th = jnp.tanh(log_a)
    u = jnp.sqrt(-2.0 * th / (1.0 - th)) * (ig * xc)
    sub = lax.broadcasted_iota(jnp.int32, xc.shape, 0) & (SUBLANES - 1)
    k = 1
    while k < SUBLANES:
        ok = (sub < SUBLANES - k) if reverse else (sub >= k)
        shift = T - k if reverse else k
        a_n = jnp.where(ok, pltpu.roll(a, shift, 0), 1.0)
        u_n = jnp.where(ok, pltpu.roll(u, shift, 0), 0.0)
        u = a * u_n + u
        a = a * a_n
        k *= 2
    n_groups = T // SUBLANES
    last = 0 if reverse else SUBLANES - 1
    carry = jnp.broadcast_to(h_ref[d, 0:1, :], (SUBLANES, LRU_W))
    pieces = [None] * n_groups
    for g in (range(n_groups - 1, -1, -1) if reverse else range(n_groups)):
        rows = slice(g * SUBLANES, (g + 1) * SUBLANES)
        pieces[g] = a[rows] * carry + u[rows]
        carry = jnp.broadcast_to(pieces[g][last:last + 1, :], (SUBLANES, LRU_W))
    h_ref[d, 0:1, :] = carry[0:1, :]
    return jnp.concatenate(pieces, axis=0)


def _lru_kernel(xf_ref, pf_ref, nf_ref, xb_ref, pb_ref, nb_ref, cw_ref, cb_ref, wg_ref, bias_ref,
                lam_ref, of_ref, ob_ref, h_ref, *, n_lat, n_ctx, prec):
    i = pl.program_id(1)

    @pl.when(i == 0)
    def _():
        h_ref[...] = jnp.zeros_like(h_ref)

    cw = cw_ref[...]
    cb = cb_ref[...]
    for d, (x_ref, p_ref, n_ref, o_ref) in enumerate(
            ((xf_ref, pf_ref, nf_ref, of_ref), (xb_ref, pb_ref, nb_ref, ob_ref))):
        blk = _bwd_blk(i, n_lat, n_ctx) if d else _fwd_blk(i, n_lat, n_ctx)
        has_prev = (blk != 0) & (blk != n_lat)
        has_next = (blk != n_lat - 1) & (blk != n_lat + n_ctx - 1)
        xc = _lru_conv(x_ref[0], p_ref[0], n_ref[0], has_prev, has_next, cw, cb)
        sp_lam = _softplus(-lam_ref[d])
        o_ref[0] = _lru_dir(d, xc, wg_ref[d], bias_ref[d], sp_lam, h_ref, bool(d), prec)


def _lru(proj, conv_w, conv_b, wg, bias, lam, n_lat, n_ctx, prec=1):
    b, ltot, _ = proj.shape
    T = T_LRU
    r8 = T // 8
    n8 = ltot // 8
    fwd = functools.partial(_fwd_blk, n_lat=n_lat, n_ctx=n_ctx)
    bwd = functools.partial(_bwd_blk, n_lat=n_lat, n_ctx=n_ctx)
    col8 = OD_XB
    cur = lambda order: pl.BlockSpec((1, T, LRU_W), lambda bi, i: (bi, order(i), OD_XB))
    prev = lambda order: pl.BlockSpec(
        (1, 8, LRU_W), lambda bi, i: (bi, jnp.maximum(order(i) * r8 - 1, 0), col8))
    nxt = lambda order: pl.BlockSpec(
        (1, 8, LRU_W), lambda bi, i: (bi, jnp.minimum((order(i) + 1) * r8, n8 - 1), col8))
    full = lambda shape: pl.BlockSpec(shape, lambda bi, i: (0,) * len(shape))
    out = jax.ShapeDtypeStruct((b, ltot, LRU_W), F32)
    return pl.pallas_call(
        functools.partial(_lru_kernel, n_lat=n_lat, n_ctx=n_ctx, prec=prec),
        grid=(b, n_lat + n_ctx),
        in_specs=[
            cur(fwd), prev(fwd), nxt(fwd), cur(bwd), prev(bwd), nxt(bwd),
            full((4, LRU_W)), full((1, LRU_W)), full((2, LRU_W, 2 * LRU_W)), full((2, 1, 2 * LRU_W)),
            full((2, 1, LRU_W)),
        ],
        out_specs=[
            pl.BlockSpec((1, T, LRU_W), lambda bi, i: (bi, fwd(i), 0)),
            pl.BlockSpec((1, T, LRU_W), lambda bi, i: (bi, bwd(i), 0)),
        ],
        out_shape=[out, out],
        scratch_shapes=[pltpu.VMEM((2, 8, LRU_W), F32)],
        compiler_params=pltpu.CompilerParams(
            dimension_semantics=("arbitrary", "arbitrary"), vmem_limit_bytes=VMEM_LIMIT),
        name="rglru_scan",
    )(proj, proj, proj, proj, proj, proj, conv_w, conv_b, wg, bias, lam)


def _head_rms(x, heads, width):
    outs = []
    for h in range(heads):
        xh = x[:, h * width:(h + 1) * width]
        outs.append(xh * lax.rsqrt(jnp.mean(xh * xh, axis=-1, keepdims=True) + EPS))
    return jnp.concatenate(outs, axis=1)


def _ffn_rows(h, m, g, w1_ref, w3_ref, w2_ref):
    v = _norm_mod(h, g, m[3:4], m[4:5]).astype(BF16)
    a = jnp.dot(v, w1_ref[...], preferred_element_type=F32)
    c = jnp.dot(v, w3_ref[...], preferred_element_type=F32)
    hid = (_silu(a) * c).astype(BF16)
    return h + m[5:6] * jnp.dot(hid, w2_ref[...], preferred_element_type=F32)


def _resident(shape):
    return pl.BlockSpec(shape, lambda i, t: (0,) * len(shape), pipeline_mode=pl.Buffered(1))


def _mix_ffn_even_kernel(x_ref, c_ref, rf_ref, rb_ref, gd_ref, mf_ref, mb_ref, op_ref, g2_ref, ng_ref, w_ref,
                         n2_ref, mod_ref, w1_ref, w3_ref, w2_ref, o_ref, *, n_lat_tiles):
    m = mod_ref[0]
    g = _mm(_sigmoid(gd_ref[0]), g2_ref[...], NN, 2, 2)
    y1 = (rf_ref[0] + rb_ref[0]) * g
    y2 = _head_rms(mf_ref[0] + mb_ref[0], MLSTM_HEADS, MLSTM_HD) * ng_ref[...] * _sigmoid(op_ref[0])
    y = jnp.concatenate([y1, y2], axis=1).astype(BF16)
    h = _lat_or_ctx(x_ref, c_ref, n_lat_tiles) + m[2:3] * jnp.dot(y, w_ref[...], preferred_element_type=F32)
    o_ref[0] = _ffn_rows(h, m, n2_ref[...], w1_ref, w3_ref, w2_ref)


def _mix_ffn_even(x, ctx, rf, rb, proj, mf, mb, g2, ng, w_out, n2g, mod, w1, w3, w2, n_lat_tiles):
    b, seq, d = x.shape
    ff = w1.shape[1]
    nt = n_lat_tiles + 1
    tok = lambda width, col=0: pl.BlockSpec((1, TM, width), lambda i, t: (i, t, col))
    return pl.pallas_call(
        functools.partial(_mix_ffn_even_kernel, n_lat_tiles=n_lat_tiles),
        grid=(b, nt),
        in_specs=_lat_ctx_specs(d, n_lat_tiles) + [
            tok(RWKV_W), tok(RWKV_W), tok(128, EV_GD), tok(MLSTM_W), tok(MLSTM_W), tok(MLSTM_W, EV_O),
            _resident((GATE_LORA, RWKV_W)), _resident((1, MLSTM_W)), _resident((d, d)), _resident((1, d)),
            pl.BlockSpec((1, 6, d), lambda i, t: (jnp.where(t >= n_lat_tiles, 4, i), 0, 0)),
            _resident((d, ff)), _resident((d, ff)), _resident((ff, d)),
        ],
        out_specs=tok(d),
        out_shape=jax.ShapeDtypeStruct((b, nt * TM, d), F32),
        compiler_params=pltpu.CompilerParams(vmem_limit_bytes=VMEM_LIMIT),
        name="mix_ffn_even",
    )(x, ctx, rf, rb, proj, mf, mb, proj, g2, ng, w_out, n2g, mod, w1, w3, w2)


def _gelu_tanh(x):
    return 0.5 * x * (1.0 + jnp.tanh(0.7978845608028654 * (x + 0.044715 * x * x * x)))


def _gather_rows(ref):
    return jnp.concatenate([ref[0, :, r, :] for r in range(ref.shape[2])], axis=0)


def _mix_ffn_odd_kernel(h_ref, hf_ref, hb_ref, g_ref, lf_ref, lb_ref, gb_ref, ng_ref, w_ref, n2_ref, mod_ref,
                        w1_ref, w3_ref, w2_ref, fg_ref, o_ref):
    m = mod_ref[0]
    hsum = _gather_rows(hf_ref) + _gather_rows(hb_ref)
    y1 = _head_rms(hsum, HGRN_HEADS, HGRN_FD) * ng_ref[...] * _silu(_gather_rows(g_ref))
    y2 = (_gather_rows(lf_ref) + _gather_rows(lb_ref)) * _gelu_tanh(_gather_rows(gb_ref))
    y = jnp.concatenate([y1, y2], axis=1).astype(BF16)
    h = h_ref[0] + m[2:3] * jnp.dot(y, w_ref[...], preferred_element_type=F32)
    for lo in range(0, h.shape[0], TM):
        out = _ffn_rows(h[lo:lo + TM], m, n2_ref[...], w1_ref, w3_ref, w2_ref)
        ms = jnp.mean(out * out, axis=-1, keepdims=True)
        o_ref[0, lo:lo + TM, :] = out * lax.rsqrt(ms + EPS) * fg_ref[...]


def _mix_ffn_odd(h_all, hf, hb, proj, lf, lb, ng, w_out, n2g, mod, w1, w3, w2, final_g, rows):
    b, ltot, d = h_all.shape
    ff = w1.shape[1]
    tr = TM_ODD // GRID_W
    cm = lambda a: a.reshape(b, ltot // rows, rows, a.shape[-1])
    col = lambda width, c=0: pl.BlockSpec((1, GRID_W, tr, width), lambda i, t: (i, 0, t, c))
    hspec = pl.BlockSpec((1, TM_ODD, d), lambda i, t: (i, t, 0))
    return pl.pallas_call(
        _mix_ffn_odd_kernel,
        grid=(b, rows * GRID_W // TM_ODD),
        in_specs=[
            hspec, col(HGRN_W), col(HGRN_W), col(HGRN_W, OD_G), col(LRU_W), col(LRU_W), col(LRU_W, OD_GB),
            _resident((1, HGRN_W)), _resident((d, d)), _resident((1, d)),
            pl.BlockSpec((1, 6, d), lambda i, t: (i, 0, 0)),
            _resident((d, ff)), _resident((d, ff)), _resident((ff, d)), _resident((1, d)),
        ],
        out_specs=hspec,
        out_shape=jax.ShapeDtypeStruct((b, rows * GRID_W, d), F32),
        compiler_params=pltpu.CompilerParams(vmem_limit_bytes=VMEM_LIMIT),
        name="mix_ffn_odd",
    )(h_all, cm(hf), cm(hb), cm(proj), cm(lf), cm(lb), cm(proj), ng, w_out, n2g, mod, w1, w3, w2, final_g)


def _pack_even_in_w(w):
    r, k, v, wd, ad, gd, q, mk, mv, mo, ip, fp = jnp.split(
        w, [512, 1024, 1536, 1664, 1792, 1920, 2432, 2944, 3456, 3968, 3976], axis=1)
    lora = jnp.concatenate([wd[:, :LORA], ad[:, :LORA], wd[:, LORA:], ad[:, LORA:]], axis=1)
    gates = jnp.concatenate([ip, fp, jnp.zeros((w.shape[0], 128 - 16), w.dtype)], axis=1)
    return jnp.concatenate([r, k, v, q, mk, mv, mo, lora, gd, gates], axis=1).astype(BF16)


def _pack_rwkv_params(mu, w0, w2, a0, a2, kk, ka, rk, lnw, lnb):
    mu_x = mu[:, None, :3 * RWKV_W]
    mu_lo = mu[:, None, 3 * RWKV_W:]
    p512 = jnp.stack([w0, a0, kk, ka, lnw, lnb, rk.reshape(2, RWKV_W), jnp.zeros_like(w0)], axis=1)
    z = jnp.zeros_like(w2)
    wcomb = jnp.concatenate([jnp.concatenate([w2, z], axis=2), jnp.concatenate([z, a2], axis=2)], axis=1)
    return mu_x, mu_lo, p512, wcomb


def _pack_lru_params(wa, ba, wi, bi):
    def dense(w):
        eye = jnp.eye(LRU_BLOCKS, dtype=w.dtype)
        return jnp.einsum('dhij,hg->dhigj', w, eye).reshape(2, LRU_W, LRU_W)
    wg = jnp.concatenate([dense(wa), dense(wi)], axis=2)
    bias = jnp.concatenate([ba, bi], axis=1)[:, None, :]
    return wg, bias


def kernel(x, c, ctx, c_ctx, norm1_g, norm2_g, mod_w, mod_b, ffn_w1, ffn_w3, ffn_w2, final_g, ev_in_w, ev_out_w, rwkv_mu, rwkv_w0, rwkv_w2, rwkv_a0, rwkv_a2, rwkv_kk, rwkv_ka, rwkv_rk, rwkv_lnw, rwkv_lnb, rwkv_g2, mlstm_bi, mlstm_bf, mlstm_ng, od_in_w, od_out_w, hgrn_lb, hgrn_ng, lru_conv_w, lru_conv_b, lru_wa, lru_ba, lru_wi, lru_bi, lru_lam):
    b, seq, d = x.shape
    n_ctx_tok = ctx.shape[1]
    rows = seq // GRID_W
    n_lat_tiles = seq // TM
    depth = mod_w.shape[0]

    assert depth == 2, "layer 0 reads x / ctx directly and the odd layer is the last one"
    c8 = jnp.concatenate([c, c_ctx[None, :], jnp.zeros((8 - b - 1, d), F32)], axis=0)
    mod = _modulation(c8, mod_w, mod_b).reshape(depth, 8, 6, d)

    h_all = None
    for l in range(depth):
        j = l // 2
        g1 = norm1_g[l][None, :]
        ffn_w = (ffn_w1[l].astype(BF16), ffn_w3[l].astype(BF16), ffn_w2[l].astype(BF16))
        if l % 2 == 0:
            proj = _in_proj_even(x, ctx, g1, mod[l], _pack_even_in_w(ev_in_w[j]), n_lat_tiles)
            mu_x, mu_lo, p512, wcomb = _pack_rwkv_params(
                rwkv_mu[j], rwkv_w0[j], rwkv_w2[j], rwkv_a0[j], rwkv_a2[j], rwkv_kk[j], rwkv_ka[j],
                rwkv_rk[j], rwkv_lnw[j], rwkv_lnb[j])
            rf, rb = _rwkv(proj, mu_x, mu_lo, p512, wcomb, seq // T_RWKV, n_ctx_tok // T_RWKV, bb=4)
            gbias = jnp.concatenate([mlstm_bi[j].reshape(-1), mlstm_bf[j].reshape(-1),
                                     jnp.zeros((128 - 4 * MLSTM_HEADS,), F32)])[None, :]
            mf, mb = _mlstm(proj, gbias, seq // T_MLSTM, n_ctx_tok // T_MLSTM, bb=2)
            h_all = _mix_ffn_even(x, ctx, rf, rb, proj, mf, mb, rwkv_g2[j], mlstm_ng[j][None, :],
                                  ev_out_w[j].astype(BF16), norm2_g[l][None, :], mod[l], *ffn_w, n_lat_tiles)
        else:
            proj = _in_proj_odd(h_all, g1, mod[l], od_in_w[j].astype(BF16), rows)
            hf, hb = _hgrn(proj, hgrn_lb, l, seq // T_HGRN, n_ctx_tok // T_HGRN, bb=4)
            wg, bias = _pack_lru_params(lru_wa[j], lru_ba[j], lru_wi[j], lru_bi[j])
            lf, lb = _lru(proj, lru_conv_w[j], lru_conv_b[j][None, :], wg, bias, lru_lam[j][:, None, :],
                          seq // T_LRU, n_ctx_tok // T_LRU)
            h_all = _mix_ffn_odd(h_all, hf, hb, proj, lf, lb, hgrn_ng[j][None, :], od_out_w[j].astype(BF16),
                                 norm2_g[l][None, :], mod[l], *ffn_w, final_g[None, :], rows)
    return h_all
```

```python
import functools

import jax
import jax.numpy as jnp
from jax import lax
from jax.experimental import pallas as pl
from jax.experimental.pallas import tpu as pltpu

F32 = jnp.float32
BF16 = jnp.bfloat16

D_MODEL = 1024
GRID_W = 64
D_FF = 2816
EPS = 1e-6

RWKV_HEADS = 8
RWKV_HD = 64
RWKV_W = RWKV_HEADS * RWKV_HD
LORA = 64
GATE_LORA = 128
RWKV_GN_EPS = 64e-5
MLSTM_HEADS = 4
MLSTM_HD = 128
MLSTM_W = MLSTM_HEADS * MLSTM_HD
HGRN_HEADS = 4
HGRN_FD = 128
HGRN_W = HGRN_HEADS * HGRN_FD
LRU_W = 512
LRU_BLOCKS = 8
LRU_BD = LRU_W // LRU_BLOCKS
LRU_C = 8.0
SUBLANES = 8

T_RWKV = 64
T_MLSTM = 128
T_HGRN = 64
T_LRU = 256
TM = 256
TM_ODD = 8 * GRID_W

EV_N = 4096
EV_RKV, EV_Q, EV_K, EV_V, EV_O = 0, 3, 4, 5, 6
EV_LORA, EV_GD, EV_GATES = 28, 30, 31
OD_N = 3584
OD_Q, OD_F, OD_I, OD_G, OD_XB, OD_GB = 0, 1, 3, 4, 5, 6

VMEM_LIMIT = 56 * 1024 * 1024

NT = ((1,), (1,))
NN = ((1,), (0,))
TN = ((0,), (0,))


def _split(x, n):
    parts = []
    r = x
    for i in range(n):
        p = r.astype(BF16)
        parts.append(p)
        if i + 1 < n:
            r = r - p.astype(F32)
    return parts


def _mm(a, b, dims=NN, pa=1, pb=1):
    dn = (dims, ((), ()))
    ap = _split(a, pa) if a.dtype != BF16 else [a]
    bp = _split(b, pb) if b.dtype != BF16 else [b]
    n = max(len(ap), len(bp))
    acc = None
    for i, x in enumerate(ap):
        for j, y in enumerate(bp):
            if i + j < n:
                t = lax.dot_general(x, y, dn, preferred_element_type=F32)
                acc = t if acc is None else acc + t
    return acc


def _sigmoid(x):
    return 1.0 / (1.0 + jnp.exp(-x))


def _silu(x):
    return x * _sigmoid(x)


def _log_sigmoid(x):
    return jnp.minimum(x, 0.0) - jnp.log1p(jnp.exp(-jnp.abs(x)))


def _softplus(x):
    return jnp.maximum(x, 0.0) + jnp.log1p(jnp.exp(-jnp.abs(x)))


def _masks(T, reverse):
    ti = lax.broadcasted_iota(jnp.int32, (T, T), 0)
    si = lax.broadcasted_iota(jnp.int32, (T, T), 1)
    if reverse:
        return si >= ti, si > ti
    return si <= ti, si < ti


def _ones_mask(incl):
    return jnp.where(incl, 1.0, 0.0).astype(BF16)


def _shift_prev(x, carry_row, reverse):
    T = x.shape[0]
    row = lax.broadcasted_iota(jnp.int32, x.shape, 0)
    if reverse:
        return jnp.where(row == T - 1, carry_row, pltpu.roll(x, T - 1, 0))
    return jnp.where(row == 0, carry_row, pltpu.roll(x, 1, 0))


def _fwd_blk(i, n_lat, n_ctx):
    return jnp.where(i < n_ctx, n_lat + i, i - n_ctx)


def _bwd_blk(i, n_lat, n_ctx):
    return n_lat + n_ctx - 1 - i


def _mod_kernel(c_ref, w_ref, b_ref, o_ref):
    c = c_ref[...]
    o_ref[0] = _mm(_silu(c), w_ref[0], NN, 3, 3) + b_ref[0]


def _modulation(c8, mod_w, mod_b):
    depth, d, n = mod_w.shape
    return pl.pallas_call(
        _mod_kernel,
        grid=(depth, n // d),
        in_specs=[
            pl.BlockSpec((8, d), lambda l, j: (0, 0)),
            pl.BlockSpec((1, d, d), lambda l, j: (l, 0, j)),
            pl.BlockSpec((1, 1, d), lambda l, j: (l, 0, j)),
        ],
        out_specs=pl.BlockSpec((1, 8, d), lambda l, j: (l, 0, j)),
        out_shape=jax.ShapeDtypeStruct((depth, 8, n), F32),
        compiler_params=pltpu.CompilerParams(vmem_limit_bytes=VMEM_LIMIT),
        name="adaln_modulation",
    )(c8, mod_w, mod_b.reshape(depth, 1, n))


def _norm_mod(x, g, shift, scale):
    ms = jnp.mean(x * x, axis=-1, keepdims=True)
    return (x * lax.rsqrt(ms + EPS) * g) * (1.0 + scale) + shift


def _lat_or_ctx(x_ref, c_ref, n_lat_tiles):
    return jnp.where(pl.program_id(1) < n_lat_tiles, x_ref[0], c_ref[0])


def _lat_ctx_specs(d, n_lat_tiles):
    return [
        pl.BlockSpec((1, TM, d), lambda i, t: (i, jnp.minimum(t, n_lat_tiles - 1), 0)),
        pl.BlockSpec((1, TM, d), lambda i, t: (i, 0, 0)),
    ]


def _in_even_kernel(x_ref, c_ref, g_ref, mod_ref, w_ref, o_ref, *, n_lat_tiles):
    m = mod_ref[0]
    u = _norm_mod(_lat_or_ctx(x_ref, c_ref, n_lat_tiles), g_ref[...], m[0:1], m[1:2])
    o_ref[0] = jnp.dot(u.astype(BF16), w_ref[...], preferred_element_type=F32)


def _in_proj_even(x, ctx, g, mod, w, n_lat_tiles):
    b, seq, d = x.shape
    assert ctx.shape[1] == TM
    n = w.shape[1]
    nt = n_lat_tiles + 1
    return pl.pallas_call(
        functools.partial(_in_even_kernel, n_lat_tiles=n_lat_tiles),
        grid=(b, nt),
        in_specs=_lat_ctx_specs(d, n_lat_tiles) + [
            pl.BlockSpec((1, d), lambda i, t: (0, 0)),
            pl.BlockSpec((1, 6, d), lambda i, t: (jnp.where(t >= n_lat_tiles, 4, i), 0, 0)),
            pl.BlockSpec((d, n), lambda i, t: (0, 0)),
        ],
        out_specs=pl.BlockSpec((1, TM, n), lambda i, t: (i, t, 0)),
        out_shape=jax.ShapeDtypeStruct((b, nt * TM, n), F32),
        compiler_params=pltpu.CompilerParams(vmem_limit_bytes=VMEM_LIMIT),
        name="in_proj_even",
    )(x, ctx, g, mod, w)


def _in_odd_kernel(x_ref, g_ref, mod_ref, w_ref, *rest, scatter):
    o_ref = rest[-1]
    m = mod_ref[0]
    if scatter:
        x = jnp.concatenate([x_ref[0, :, w, :] for w in range(GRID_W)], axis=0)
    else:
        x = x_ref[0]
    u = _norm_mod(x, g_ref[...], m[0:1], m[1:2])
    y = jnp.dot(u.astype(BF16), w_ref[...], preferred_element_type=F32)
    if scatter:
        o_ref[0] = y.reshape(o_ref.shape[1:])
    else:
        for j in range(o_ref.shape[1]):
            o_ref[0, j] = y[j * o_ref.shape[2]:(j + 1) * o_ref.shape[2], :]


def _in_proj_odd(h_all, g, mod, w, rows):
    b, ltot, d = h_all.shape
    n = w.shape[1]
    lat = rows * GRID_W
    n_ctx_slots = (ltot - lat) // rows
    tr = TM_ODD // GRID_W
    out_shape = jax.ShapeDtypeStruct((b, GRID_W + n_ctx_slots, rows, n), F32)
    common = [
        pl.BlockSpec((1, d), lambda i, t: (0, 0)),
        None,
        pl.BlockSpec((d, n), lambda i, t: (0, 0)),
    ]
    lat_specs = list(common)
    lat_specs[1] = pl.BlockSpec((1, 6, d), lambda i, t: (i, 0, 0))
    proj = pl.pallas_call(
        functools.partial(_in_odd_kernel, scatter=True),
        grid=(b, lat // TM_ODD),
        in_specs=[pl.BlockSpec((1, tr, GRID_W, d), lambda i, t: (i, t, 0, 0))] + lat_specs,
        out_specs=pl.BlockSpec((1, GRID_W, tr, n), lambda i, t: (i, 0, t, 0)),
        out_shape=out_shape,
        compiler_params=pltpu.CompilerParams(vmem_limit_bytes=VMEM_LIMIT),
        name="in_proj_odd",
    )(h_all.reshape(b, ltot // GRID_W, GRID_W, d), g, mod, w)
    ctx_tok = ltot - lat
    ctx_specs = list(common)
    ctx_specs[1] = pl.BlockSpec((1, 6, d), lambda i, t: (4, 0, 0))
    return pl.pallas_call(
        functools.partial(_in_odd_kernel, scatter=False),
        grid=(b, 1),
        in_specs=[pl.BlockSpec((1, ctx_tok, d), lambda i, t: (i, lat // ctx_tok, 0))] + ctx_specs
        + [pl.BlockSpec(memory_space=pl.ANY)],
        out_specs=pl.BlockSpec((1, n_ctx_slots, rows, n), lambda i, t: (i, GRID_W // n_ctx_slots, 0, 0)),
        out_shape=out_shape,
        input_output_aliases={4: 0},
        compiler_params=pltpu.CompilerParams(vmem_limit_bytes=VMEM_LIMIT),
        name="in_proj_odd_ctx",
    )(h_all, g, mod, w, proj).reshape(b, ltot, n)


def _head_sums(parts, ones_blk):
    T = parts[0].shape[0]
    s = _mm(jnp.concatenate(parts, axis=0), ones_blk, NN, 1, 1)
    return [s[i * T:(i + 1) * T] for i in range(len(parts))]


def _rwkv_shift(x, lo, carry_x, carry_lo, mu_x, mu_lo, reverse):
    xs = x + mu_x * (_shift_prev(x, carry_x, reverse) - x)
    los = lo + mu_lo * (_shift_prev(lo, carry_lo, reverse) - lo)
    lane = lax.broadcasted_iota(jnp.int32, los.shape, 1)
    return xs, jnp.where(lane < LORA, jnp.tanh(los), los)


def _rwkv_operands(xs, wa, ss, p512, reverse):
    T = xs.shape[0]
    incl, _ = _masks(T, reverse)
    edge = 0 if reverse else T - 1
    r = xs[:, :RWKV_W]
    k = xs[:, RWKV_W:2 * RWKV_W]
    v = xs[:, 2 * RWKV_W:]
    w0, a0, k_k, k_a = (p512[i:i + 1, :] for i in range(4))
    w = -_softplus(-(w0 + wa[:, :RWKV_W])) - 0.5
    a = _sigmoid(a0 + wa[:, RWKV_W:])
    logdec = -jnp.exp(w)
    kk = (k * k_k) / jnp.maximum(jnp.sqrt(ss), 1e-12)
    k2 = k * (1.0 + (a - 1.0) * k_a)
    cl = _mm(_ones_mask(incl), logdec, NN, 1, 3)
    e_pos = jnp.exp(cl)
    e_neg = jnp.exp(-cl)
    return dict(
        A=kk * jnp.exp(cl - logdec), B=kk * a * e_neg, K=k2 * e_neg, R=r * e_pos, V=v,
        gT=e_pos[edge:edge + 1, :], r=r, k2=k2)


def _rwkv_chunks(units, states, T, p):
    mm = functools.partial(_mm, pa=p, pb=p)
    pk = (lambda t: t.astype(BF16)) if p == 1 else (lambda t: t)
    hd = RWKV_HD
    pw = 2 * hd

    def bd(x):
        lo = lax.broadcasted_iota(jnp.int32, x.shape, 1) < hd
        return pk(jnp.concatenate([jnp.where(lo, x, 0.0), jnp.where(lo, 0.0, x)], axis=0))

    jobs = []
    for ui, (u, reverse) in enumerate(units):
        for pi in range(RWKV_HEADS // 2):
            sl = slice(pi * pw, (pi + 1) * pw)
            jobs.append(dict(
                ui=ui, reverse=reverse,
                AR=pk(jnp.concatenate([u['A'][:, sl], u['R'][:, sl]], axis=0)),
                B=u['B'][:, sl], K=u['K'][:, sl], V=u['V'][:, sl], gT=u['gT'][:, sl],
                S=states[ui][pi]))
    ti = lax.broadcasted_iota(jnp.int32, (2 * T, pw), 0)
    si = lax.broadcasted_iota(jnp.int32, (2 * T, pw), 1) & (hd - 1)
    t1 = lax.broadcasted_iota(jnp.int32, (T, pw), 0)
    s1 = lax.broadcasted_iota(jnp.int32, (T, pw), 1) & (hd - 1)
    masks = {}
    for reverse in (False, True):
        if reverse:
            masks[reverse] = (si > jnp.where(ti < T, ti, ti - T - 1), s1 > t1, s1 >= t1)
        else:
            masks[reverse] = (si < jnp.where(ti < T, ti, ti - T + 1), s1 < t1, s1 <= t1)
    for j in jobs:
        g = mm(j['AR'], jnp.concatenate([bd(j['B']), bd(j['K']), bd(j['S'])], axis=0), NT)
        j['GB'], j['GK'], j['SS'] = g[:, :pw], g[:, pw:2 * pw], g[:, 2 * pw:]
    for j in jobs:
        both, strict, _ = masks[j['reverse']]
        j['LV'] = mm(jnp.where(both, j['GK'], 0.0), bd(j['V']))
        j['N'] = jnp.where(strict, j['GB'][:T], 0.0)
    rc = t1 ^ s1
    eye = jnp.where(rc == 0, 1.0, 0.0)
    for j in jobs:
        j['D'] = eye - jnp.where(rc == 1, j['N'], 0.0)
    s = 2
    while s < T:
        level = (rc >= s) & (rc < 2 * s)
        for j in jobs:
            j['LD'] = mm(jnp.where(level, j['N'], 0.0), bd(j['D']))
        for j in jobs:
            j['D'] = j['D'] - mm(j['D'], bd(j['LD']))
        s *= 2
    for j in jobs:
        j['X'] = mm(j['D'], bd(j['SS'][:T] + j['LV'][:T]))
    for j in jobs:
        incl = masks[j['reverse']][2]
        j['O'] = j['SS'][T:] + j['LV'][T:] - mm(jnp.where(incl, j['GB'][T:], 0.0), bd(j['X']))
    lo = lax.broadcasted_iota(jnp.int32, (hd, pw), 1) < hd
    for j in jobs:
        full = mm(jnp.concatenate([j['V'], -j['X']], axis=0), jnp.concatenate([j['K'], j['B']], axis=0), TN)
        j['S_new'] = (j['S'] + jnp.where(lo, full[:hd], full[hd:])) * j['gT']
    outs, new_states = [], []
    for ui in range(len(units)):
        mine = [j for j in jobs if j['ui'] == ui]
        outs.append(jnp.concatenate([j['O'] for j in mine], axis=1))
        new_states.append([j['S_new'] for j in mine])
    return outs, new_states


def _rwkv_kernel(xf_ref, xb_ref, lf_ref, lb_ref, mux_ref, mulo_ref, p512_ref, wcomb_ref, ones_ref,
                 of_ref, ob_ref, s_ref, cx_ref, cl_ref, *, n_ctx, prec):
    i = pl.program_id(1)
    T = xf_ref.shape[1]

    @pl.when(i == 0)
    def _():
        s_ref[...] = jnp.zeros_like(s_ref)

    @pl.when((i == 0) | (i == n_ctx))
    def _():
        cx_ref[...] = jnp.zeros_like(cx_ref)
        cl_ref[...] = jnp.zeros_like(cl_ref)

    ones_blk = ones_ref[...]
    refs = ((xf_ref, lf_ref, of_ref), (xb_ref, lb_ref, ob_ref))
    nb = xf_ref.shape[0]
    where = [(bi, d) for d in range(2) for bi in range(nb)]
    xs, zs = [], []
    for bi, d in where:
        x = refs[d][0][bi]
        lo = refs[d][1][bi]
        edge = 0 if d else T - 1
        xs_u, z_u = _rwkv_shift(x, lo, cx_ref[bi, d, 0:1, :], cl_ref[bi, d, 0:1, :], mux_ref[d],
                                mulo_ref[d], bool(d))
        cx_ref[bi, d, 0:1, :] = x[edge:edge + 1, :]
        cl_ref[bi, d, 0:1, :] = lo[edge:edge + 1, :]
        xs.append(xs_u)
        zs.append(z_u)
    was = []
    for d in range(2):
        wa = _mm(jnp.concatenate(zs[d * nb:(d + 1) * nb], axis=0), wcomb_ref[d], NN, 2, 2)
        was += [wa[bi * T:(bi + 1) * T] for bi in range(nb)]
    kk0 = [x_u[:, RWKV_W:2 * RWKV_W] * p512_ref[d][2:3, :] for x_u, (_, d) in zip(xs, where)]
    sss = _head_sums([k0 * k0 for k0 in kk0], ones_blk)
    units = [(_rwkv_operands(x_u, wa, ss, p512_ref[d], bool(d)), bool(d))
             for x_u, wa, ss, (_, d) in zip(xs, was, sss, where)]
    states = [[s_ref[bi, d, h] for h in range(RWKV_HEADS // 2)] for bi, d in where]
    outs, new_states = _rwkv_chunks(units, states, T, prec)
    for (bi, d), ns in zip(where, new_states):
        for h in range(RWKV_HEADS // 2):
            s_ref[bi, d, h] = ns[h]
    inv_n = 1.0 / RWKV_HD
    means = _head_sums(outs, ones_blk)
    ocs = [o - mean * inv_n for o, mean in zip(outs, means)]
    varis = _head_sums([oc * oc for oc in ocs], ones_blk)
    rks = _head_sums([u['r'] * u['k2'] * p512_ref[d][6:7, :] for (u, _), (_, d) in zip(units, where)], ones_blk)
    for (bi, d), (u, _), oc, var, rk in zip(where, units, ocs, varis, rks):
        ln_w, ln_b = p512_ref[d][4:5, :], p512_ref[d][5:6, :]
        gn = oc * lax.rsqrt(var * inv_n + RWKV_GN_EPS) * ln_w + ln_b
        refs[d][2][bi] = gn + rk * u['V']


def _rwkv(proj, mu_x, mu_lo, p512, wcomb, n_lat, n_ctx, prec=1, bb=1):
    b, ltot, _ = proj.shape
    T = T_RWKV
    ones_blk = jnp.kron(jnp.eye(RWKV_HEADS, dtype=F32), jnp.ones((RWKV_HD, RWKV_HD), F32)).astype(BF16)
    fwd = functools.partial(_fwd_blk, n_lat=n_lat, n_ctx=n_ctx)
    bwd = functools.partial(_bwd_blk, n_lat=n_lat, n_ctx=n_ctx)
    full = lambda shape: pl.BlockSpec(shape, lambda bi, i: (0,) * len(shape))
    out = jax.ShapeDtypeStruct((b, ltot, RWKV_W), F32)
    return pl.pallas_call(
        functools.partial(_rwkv_kernel, n_ctx=n_ctx, prec=prec),
        grid=(b // bb, n_lat + n_ctx),
        in_specs=[
            pl.BlockSpec((bb, T, 3 * RWKV_W), lambda bi, i: (bi, fwd(i), EV_RKV)),
            pl.BlockSpec((bb, T, 3 * RWKV_W), lambda bi, i: (bi, bwd(i), EV_RKV)),
            pl.BlockSpec((bb, T, 128), lambda bi, i: (bi, fwd(i), EV_LORA)),
            pl.BlockSpec((bb, T, 128), lambda bi, i: (bi, bwd(i), EV_LORA + 1)),
            full((2, 1, 3 * RWKV_W)), full((2, 1, 128)), full((2, 8, RWKV_W)),
            full((2, 128, 2 * RWKV_W)), full((RWKV_W, RWKV_W)),
        ],
        out_specs=[
            pl.BlockSpec((bb, T, RWKV_W), lambda bi, i: (bi, fwd(i), 0)),
            pl.BlockSpec((bb, T, RWKV_W), lambda bi, i: (bi, bwd(i), 0)),
        ],
        out_shape=[out, out],
        scratch_shapes=[
            pltpu.VMEM((bb, 2, RWKV_HEADS // 2, RWKV_HD, 2 * RWKV_HD), F32),
            pltpu.VMEM((bb, 2, 8, 3 * RWKV_W), F32),
            pltpu.VMEM((bb, 2, 8, 128), F32),
        ],
        compiler_params=pltpu.CompilerParams(
            dimension_semantics=("arbitrary", "arbitrary"), vmem_limit_bytes=VMEM_LIMIT),
        name="rwkv7_chunked",
    )(proj, proj, proj, proj, mu_x, mu_lo, p512, wcomb, ones_blk)


def _mlstm_chunks(units, gbias, c_ref, n_ref, m_ref, p):
    mm = functools.partial(_mm, pa=p, pb=p)
    pk = (lambda t: t.astype(BF16)) if p == 1 else (lambda t: t)
    jobs = []
    for base, d, q, k, v, gates, reverse in units:
        T = q.shape[0]
        incl, _ = _masks(T, reverse)
        edge = 0 if reverse else T - 1
        g = gates + gbias
        lane = lax.broadcasted_iota(jnp.int32, g.shape, 1)
        g = jnp.where(lane < 2 * MLSTM_HEADS, g, _log_sigmoid(g))
        gt = g.T
        ones = _ones_mask(incl)
        b_col = _mm(ones, g, NN, 1, 3)
        b_row = _mm(gt, ones, NT, 3, 1)
        for h in range(MLSTM_HEADS):
            sl = slice(h * MLSTM_HD, (h + 1) * MLSTM_HD)
            ci = d * MLSTM_HEADS + h
            cf = 2 * MLSTM_HEADS + ci
            st = base + h
            jobs.append(dict(
                st=st, incl=incl, edge=edge,
                q=pk(q[:, sl] * (MLSTM_HD ** -0.5)), k=k[:, sl], kp=pk(k[:, sl]), v=v[:, sl],
                bc=b_col[:, cf:cf + 1], br=b_row[cf:cf + 1, :], li_c=g[:, ci:ci + 1], li_r=gt[ci:ci + 1, :],
                C0=c_ref[st], n0=n_ref[st, 0:1, :], m0=m_ref[st, 0:1, 0:1]))
    for j in jobs:
        rhs = jnp.concatenate([j['kp'], pk(j['C0']), pk(jnp.broadcast_to(j['n0'], (MLSTM_HD, MLSTM_HD)))], axis=0)
        g = mm(j['q'], rhs, NT)
        T = j['kp'].shape[0]
        j['qk'], j['qC'], j['qn'] = g[:, :T], g[:, T:T + MLSTM_HD], g[:, T + MLSTM_HD:]
    for j in jobs:
        logD = jnp.where(j['incl'], j['bc'] - j['br'] + j['li_r'], -jnp.inf)
        m_prev = j['bc'] + j['m0']
        j['m'] = jnp.maximum(m_prev, jnp.max(logD, axis=-1, keepdims=True))
        j['s'] = pk(j['qk'] * jnp.exp(logD - j['m']))
        j['inter'] = jnp.exp(m_prev - j['m'])
        bT = j['bc'][j['edge']:j['edge'] + 1, :]
        gsum = bT - j['bc'] + j['li_c']
        j['m_new'] = jnp.maximum(bT + j['m0'], jnp.max(gsum, axis=0, keepdims=True))
        j['wts'] = jnp.exp(gsum - j['m_new'])
        j['cd'] = jnp.exp(bT + j['m0'] - j['m_new'])
    ones_tt = jnp.ones((jobs[0]['s'].shape[1], MLSTM_HD), F32)
    for j in jobs:
        g = mm(j['s'], pk(jnp.concatenate([j['v'], ones_tt], axis=1)))
        j['sv'], j['rs'] = g[:, :MLSTM_HD], g[:, MLSTM_HD:]
    for j in jobs:
        j['dC'] = mm(j['wts'] * j['v'], j['kp'], TN)
    outs = []
    for j in jobs:
        num = j['sv'] + j['inter'] * j['qC']
        den = j['rs'] + j['inter'] * j['qn']
        outs.append(num / jnp.maximum(jnp.abs(den), jnp.exp(-j['m'])))
        st = j['st']
        c_ref[st] = j['cd'] * j['C0'] + j['dC']
        n_ref[st, 0:1, :] = j['cd'] * j['n0'] + jnp.sum(j['wts'] * j['k'], axis=0, keepdims=True)
        m_ref[st] = jnp.broadcast_to(j['m_new'], m_ref.shape[1:])
    return [jnp.concatenate(outs[u * MLSTM_HEADS:(u + 1) * MLSTM_HEADS], axis=1) for u in range(len(units))]


def _mlstm_kernel(qf_ref, kf_ref, vf_ref, gf_ref, qb_ref, kb_ref, vb_ref, gb_ref, gbias_ref,
                  of_ref, ob_ref, c_ref, n_ref, m_ref, *, prec):
    i = pl.program_id(1)

    @pl.when(i == 0)
    def _():
        c_ref[...] = jnp.zeros_like(c_ref)
        n_ref[...] = jnp.zeros_like(n_ref)
        m_ref[...] = jnp.zeros_like(m_ref)

    refs = ((qf_ref, kf_ref, vf_ref, gf_ref, of_ref), (qb_ref, kb_ref, vb_ref, gb_ref, ob_ref))
    units, where = [], []
    for bi in range(qf_ref.shape[0]):
        for d, (q_ref, k_ref, v_ref, g_ref, o_ref) in enumerate(refs):
            units.append(((bi * 2 + d) * MLSTM_HEADS, d, q_ref[bi], k_ref[bi], v_ref[bi], g_ref[bi], bool(d)))
            where.append((o_ref, bi))
    outs = _mlstm_chunks(units, gbias_ref[...], c_ref, n_ref, m_ref, prec)
    for (o_ref, bi), o in zip(where, outs):
        o_ref[bi] = o


def _mlstm(proj, gbias, n_lat, n_ctx, prec=1, bb=1):
    b, ltot, _ = proj.shape
    T = T_MLSTM
    fwd = functools.partial(_fwd_blk, n_lat=n_lat, n_ctx=n_ctx)
    bwd = functools.partial(_bwd_blk, n_lat=n_lat, n_ctx=n_ctx)
    spec = lambda width, col, order: pl.BlockSpec((bb, T, width), lambda bi, i: (bi, order(i), col))
    out = jax.ShapeDtypeStruct((b, ltot, MLSTM_W), F32)
    nst = bb * 2 * MLSTM_HEADS
    return pl.pallas_call(
        functools.partial(_mlstm_kernel, prec=prec),
        grid=(b // bb, n_lat + n_ctx),
        in_specs=[
            spec(MLSTM_W, EV_Q, fwd), spec(MLSTM_W, EV_K, fwd), spec(MLSTM_W, EV_V, fwd),
            spec(128, EV_GATES, fwd),
            spec(MLSTM_W, EV_Q, bwd), spec(MLSTM_W, EV_K, bwd), spec(MLSTM_W, EV_V, bwd),
            spec(128, EV_GATES, bwd),
            pl.BlockSpec((1, 128), lambda bi, i: (0, 0)),
        ],
        out_specs=[
            pl.BlockSpec((bb, T, MLSTM_W), lambda bi, i: (bi, fwd(i), 0)),
            pl.BlockSpec((bb, T, MLSTM_W), lambda bi, i: (bi, bwd(i), 0)),
        ],
        out_shape=[out, out],
        scratch_shapes=[
            pltpu.VMEM((nst, MLSTM_HD, MLSTM_HD), F32),
            pltpu.VMEM((nst, 8, MLSTM_HD), F32),
            pltpu.VMEM((nst, 8, 128), F32),
        ],
        compiler_params=pltpu.CompilerParams(
            dimension_semantics=("arbitrary", "arbitrary"), vmem_limit_bytes=VMEM_LIMIT),
        name="mlstm_chunked",
    )(proj, proj, proj, proj, proj, proj, proj, proj, gbias)


def _hgrn_chunks(units, lb, s_ref, p):
    mm = functools.partial(_mm, pa=p, pb=p)
    jobs = []
    for base, q, fpre, vi, reverse in units:
        T = q.shape[0]
        incl, _ = _masks(T, reverse)
        edge = 0 if reverse else T - 1
        f = lb + (1.0 - lb) * _sigmoid(fpre)
        kk = 1.0 - f
        qs = _silu(q)
        bc = _mm(_ones_mask(incl), jnp.log(f), NN, 1, 3)
        bmid = bc[T // 2:T // 2 + 1, :]
        bT = bc[edge:edge + 1, :]
        qt = qs * jnp.exp(bc - bmid)
        kt = kk * jnp.exp(bmid - bc)
        qd = qs * jnp.exp(bc)
        kd = kk * jnp.exp(bT - bc)
        eT = jnp.exp(bT)
        for h in range(HGRN_HEADS):
            sl = slice(h * HGRN_FD, (h + 1) * HGRN_FD)
            jobs.append(dict(st=base + h, incl=incl, qt=qt[:, sl], kt=kt[:, sl], qd=qd[:, sl], kd=kd[:, sl],
                             v=vi[:, sl], eT=eT[:, sl], S=s_ref[base + h]))
    for j in jobs:
        j['A'] = jnp.where(j['incl'], mm(j['qt'], j['kt'], NT), 0.0)
    for j in jobs:
        j['qS'] = mm(j['qd'], j['S'], NT)
    for j in jobs:
        j['o'] = mm(j['A'], j['v']) + j['qS']
    for j in jobs:
        s_ref[j['st']] = j['S'] * j['eT'] + mm(j['v'], j['kd'], TN)
    outs = [j['o'] for j in jobs]
    return [jnp.concatenate(outs[u * HGRN_HEADS:(u + 1) * HGRN_HEADS], axis=1) for u in range(len(units))]


def _hgrn_kernel(qf_ref, ff_ref, if_ref, qb_ref, fb_ref, ib_ref, lb_ref, of_ref, ob_ref, s_ref,
                 *, layer, prec):
    i = pl.program_id(1)

    @pl.when(i == 0)
    def _():
        s_ref[...] = jnp.zeros_like(s_ref)

    raw = lb_ref[...]
    e = jnp.exp(raw - jnp.max(raw, axis=0, keepdims=True))
    sm = e / jnp.sum(e, axis=0, keepdims=True)
    lb = jnp.sum(sm[:layer + 1], axis=0, keepdims=True) - sm[0:1]
    refs = ((qf_ref, ff_ref, if_ref, of_ref), (qb_ref, fb_ref, ib_ref, ob_ref))
    units, where = [], []
    for bi in range(qf_ref.shape[0]):
        for d, (q_ref, f_ref, v_ref, o_ref) in enumerate(refs):
            units.append(((bi * 2 + d) * HGRN_HEADS, q_ref[bi], f_ref[bi], v_ref[bi], bool(d)))
            where.append((o_ref, bi))
    outs = _hgrn_chunks(units, lb, s_ref, prec)
    for (o_ref, bi), o in zip(where, outs):
        o_ref[bi] = o


def _hgrn(proj, hgrn_lb, layer, n_lat, n_ctx, prec=1, bb=1):
    b, ltot, _ = proj.shape
    T = T_HGRN
    fwd = functools.partial(_fwd_blk, n_lat=n_lat, n_ctx=n_ctx)
    bwd = functools.partial(_bwd_blk, n_lat=n_lat, n_ctx=n_ctx)
    spec = lambda col, order: pl.BlockSpec((bb, T, HGRN_W), lambda bi, i: (bi, order(i), col))
    out = jax.ShapeDtypeStruct((b, ltot, HGRN_W), F32)
    return pl.pallas_call(
        functools.partial(_hgrn_kernel, layer=layer, prec=prec),
        grid=(b // bb, n_lat + n_ctx),
        in_specs=[
            spec(OD_Q, fwd), spec(OD_F, fwd), spec(OD_I, fwd),
            spec(OD_Q, bwd), spec(OD_F + 1, bwd), spec(OD_I, bwd),
            pl.BlockSpec(hgrn_lb.shape, lambda bi, i: (0, 0)),
        ],
        out_specs=[
            pl.BlockSpec((bb, T, HGRN_W), lambda bi, i: (bi, fwd(i), 0)),
            pl.BlockSpec((bb, T, HGRN_W), lambda bi, i: (bi, bwd(i), 0)),
        ],
        out_shape=[out, out],
        scratch_shapes=[pltpu.VMEM((bb * 2 * HGRN_HEADS, HGRN_FD, HGRN_FD), F32)],
        compiler_params=pltpu.CompilerParams(
            dimension_semantics=("arbitrary", "arbitrary"), vmem_limit_bytes=VMEM_LIMIT),
        name="hgrn2_chunked",
    )(proj, proj, proj, proj, proj, proj, hgrn_lb)


def _lru_conv(x, halo_prev, halo_next, has_prev, has_next, cw, cb):
    T = x.shape[0]
    row = lax.broadcasted_iota(jnp.int32, x.shape, 0)
    hp = jnp.where(has_prev, halo_prev, 0.0)
    hn = jnp.where(has_next, halo_next, 0.0)
    x_m1 = jnp.where(row == 0, hp[7:8, :], pltpu.roll(x, 1, 0))
    x_m2 = jnp.where(row == 0, hp[6:7, :], jnp.where(row == 1, hp[7:8, :], pltpu.roll(x, 2, 0)))
    x_p1 = jnp.where(row == T - 1, hn[0:1, :], pltpu.roll(x, T - 1, 0))
    return x_m2 * cw[0:1, :] + x_m1 * cw[1:2, :] + x * cw[2:3, :] + x_p1 * cw[3:4, :] + cb


def _lru_dir(d, xc, wg, bias, sp_lam, h_ref, reverse, p):
    T = xc.shape[0]
    gates = _sigmoid(_mm(xc, wg, NN, p, p) + bias)
    r = gates[:, :LRU_W]
    ig = gates[:, LRU_W:]
    log_a = -LRU_C * r * sp_lam
    a = jnp.exp(log_a)
    th = jnp.tanh(log_a)
    u = jnp.sqrt(-2.0 * th / (1.0 - th)) * (ig * xc)
    sub = lax.broadcasted_iota(jnp.int32, xc.shape, 0) & (SUBLANES - 1)
    k = 1
    while k < SUBLANES:
        ok = (sub < SUBLANES - k) if reverse else (sub >= k)
        shift = T - k if reverse else k
        a_n = jnp.where(ok, pltpu.roll(a, shift, 0), 1.0)
        u_n = jnp.where(ok, pltpu.roll(u, shift, 0), 0.0)
        u = a * u_n + u
        a = a * a_n
        k *= 2
    n_groups = T // SUBLANES
    last = 0 if reverse else SUBLANES - 1
    carry = jnp.broadcast_to(h_ref[d, 0:1, :], (SUBLANES, LRU_W))
    pieces = [None] * n_groups
    for g in (range(n_groups - 1, -1, -1) if reverse else range(n_groups)):
        rows = slice(g * SUBLANES, (g + 1) * SUBLANES)
        pieces[g] = a[rows] * carry + u[rows]
        carry = jnp.broadcast_to(pieces[g][last:last + 1, :], (SUBLANES, LRU_W))
    h_ref[d, 0:1, :] = carry[0:1, :]
    return jnp.concatenate(pieces, axis=0)


def _lru_kernel(xf_ref, pf_ref, nf_ref, xb_ref, pb_ref, nb_ref, cw_ref, cb_ref, wg_ref, bias_ref,
                lam_ref, of_ref, ob_ref, h_ref, *, n_lat, n_ctx, prec):
    i = pl.program_id(1)

    @pl.when(i == 0)
    def _():
        h_ref[...] = jnp.zeros_like(h_ref)

    cw = cw_ref[...]
    cb = cb_ref[...]
    for d, (x_ref, p_ref, n_ref, o_ref) in enumerate(
            ((xf_ref, pf_ref, nf_ref, of_ref), (xb_ref, pb_ref, nb_ref, ob_ref))):
        blk = _bwd_blk(i, n_lat, n_ctx) if d else _fwd_blk(i, n_lat, n_ctx)
        has_prev = (blk != 0) & (blk != n_lat)
        has_next = (blk != n_lat - 1) & (blk != n_lat + n_ctx - 1)
        xc = _lru_conv(x_ref[0], p_ref[0], n_ref[0], has_prev, has_next, cw, cb)
        sp_lam = _softplus(-lam_ref[d])
        o_ref[0] = _lru_dir(d, xc, wg_ref[d], bias_ref[d], sp_lam, h_ref, bool(d), prec)


def _lru(proj, conv_w, conv_b, wg, bias, lam, n_lat, n_ctx, prec=1):
    b, ltot, _ = proj.shape
    T = T_LRU
    r8 = T // 8
    n8 = ltot // 8
    fwd = functools.partial(_fwd_blk, n_lat=n_lat, n_ctx=n_ctx)
    bwd = functools.partial(_bwd_blk, n_lat=n_lat, n_ctx=n_ctx)
    col8 = OD_XB
    cur = lambda order: pl.BlockSpec((1, T, LRU_W), lambda bi, i: (bi, order(i), OD_XB))
    prev = lambda order: pl.BlockSpec(
        (1, 8, LRU_W), lambda bi, i: (bi, jnp.maximum(order(i) * r8 - 1, 0), col8))
    nxt = lambda order: pl.BlockSpec(
        (1, 8, LRU_W), lambda bi, i: (bi, jnp.minimum((order(i) + 1) * r8, n8 - 1), col8))
    full = lambda shape: pl.BlockSpec(shape, lambda bi, i: (0,) * len(shape))
    out = jax.ShapeDtypeStruct((b, ltot, LRU_W), F32)
    return pl.pallas_call(
        functools.partial(_lru_kernel, n_lat=n_lat, n_ctx=n_ctx, prec=prec),
        grid=(b, n_lat + n_ctx),
        in_specs=[
            cur(fwd), prev(fwd), nxt(fwd), cur(bwd), prev(bwd), nxt(bwd),
            full((4, LRU_W)), full((1, LRU_W)), full((2, LRU_W, 2 * LRU_W)), full((2, 1, 2 * LRU_W)),
            full((2, 1, LRU_W)),
        ],
        out_specs=[
            pl.BlockSpec((1, T, LRU_W), lambda bi, i: (bi, fwd(i), 0)),
            pl.BlockSpec((1, T, LRU_W), lambda bi, i: (bi, bwd(i), 0)),
        ],
        out_shape=[out, out],
        scratch_shapes=[pltpu.VMEM((2, 8, LRU_W), F32)],
        compiler_params=pltpu.CompilerParams(
            dimension_semantics=("arbitrary", "arbitrary"), vmem_limit_bytes=VMEM_LIMIT),
        name="rglru_scan",
    )(proj, proj, proj, proj, proj, proj, conv_w, conv_b, wg, bias, lam)


def _head_rms(x, heads, width):
    outs = []
    for h in range(heads):
        xh = x[:, h * width:(h + 1) * width]
        outs.append(xh * lax.rsqrt(jnp.mean(xh * xh, axis=-1, keepdims=True) + EPS))
    return jnp.concatenate(outs, axis=1)


def _ffn_rows(h, m, g, w1_ref, w3_ref, w2_ref):
    v = _norm_mod(h, g, m[3:4], m[4:5]).astype(BF16)
    a = jnp.dot(v, w1_ref[...], preferred_element_type=F32)
    c = jnp.dot(v, w3_ref[...], preferred_element_type=F32)
    hid = (_silu(a) * c).astype(BF16)
    return h + m[5:6] * jnp.dot(hid, w2_ref[...], preferred_element_type=F32)


def _resident(shape):
    return pl.BlockSpec(shape, lambda i, t: (0,) * len(shape), pipeline_mode=pl.Buffered(1))


def _mix_ffn_even_kernel(x_ref, c_ref, rf_ref, rb_ref, gd_ref, mf_ref, mb_ref, op_ref, g2_ref, ng_ref, w_ref,
                         n2_ref, mod_ref, w1_ref, w3_ref, w2_ref, o_ref, *, n_lat_tiles):
    m = mod_ref[0]
    g = _mm(_sigmoid(gd_ref[0]), g2_ref[...], NN, 2, 2)
    y1 = (rf_ref[0] + rb_ref[0]) * g
    y2 = _head_rms(mf_ref[0] + mb_ref[0], MLSTM_HEADS, MLSTM_HD) * ng_ref[...] * _sigmoid(op_ref[0])
    y = jnp.concatenate([y1, y2], axis=1).astype(BF16)
    h = _lat_or_ctx(x_ref, c_ref, n_lat_tiles) + m[2:3] * jnp.dot(y, w_ref[...], preferred_element_type=F32)
    o_ref[0] = _ffn_rows(h, m, n2_ref[...], w1_ref, w3_ref, w2_ref)


def _mix_ffn_even(x, ctx, rf, rb, proj, mf, mb, g2, ng, w_out, n2g, mod, w1, w3, w2, n_lat_tiles):
    b, seq, d = x.shape
    ff = w1.shape[1]
    nt = n_lat_tiles + 1
    tok = lambda width, col=0: pl.BlockSpec((1, TM, width), lambda i, t: (i, t, col))
    return pl.pallas_call(
        functools.partial(_mix_ffn_even_kernel, n_lat_tiles=n_lat_tiles),
        grid=(b, nt),
        in_specs=_lat_ctx_specs(d, n_lat_tiles) + [
            tok(RWKV_W), tok(RWKV_W), tok(128, EV_GD), tok(MLSTM_W), tok(MLSTM_W), tok(MLSTM_W, EV_O),
            _resident((GATE_LORA, RWKV_W)), _resident((1, MLSTM_W)), _resident((d, d)), _resident((1, d)),
            pl.BlockSpec((1, 6, d), lambda i, t: (jnp.where(t >= n_lat_tiles, 4, i), 0, 0)),
            _resident((d, ff)), _resident((d, ff)), _resident((ff, d)),
        ],
        out_specs=tok(d),
        out_shape=jax.ShapeDtypeStruct((b, nt * TM, d), F32),
        compiler_params=pltpu.CompilerParams(vmem_limit_bytes=VMEM_LIMIT),
        name="mix_ffn_even",
    )(x, ctx, rf, rb, proj, mf, mb, proj, g2, ng, w_out, n2g, mod, w1, w3, w2)


def _gelu_tanh(x):
    return 0.5 * x * (1.0 + jnp.tanh(0.7978845608028654 * (x + 0.044715 * x * x * x)))


def _gather_rows(ref):
    return jnp.concatenate([ref[0, :, r, :] for r in range(ref.shape[2])], axis=0)


def _mix_ffn_odd_kernel(h_ref, hf_ref, hb_ref, g_ref, lf_ref, lb_ref, gb_ref, ng_ref, w_ref, n2_ref, mod_ref,
                        w1_ref, w3_ref, w2_ref, fg_ref, o_ref):
    m = mod_ref[0]
    hsum = _gather_rows(hf_ref) + _gather_rows(hb_ref)
    y1 = _head_rms(hsum, HGRN_HEADS, HGRN_FD) * ng_ref[...] * _silu(_gather_rows(g_ref))
    y2 = (_gather_rows(lf_ref) + _gather_rows(lb_ref)) * _gelu_tanh(_gather_rows(gb_ref))
    y = jnp.concatenate([y1, y2], axis=1).astype(BF16)
    h = h_ref[0] + m[2:3] * jnp.dot(y, w_ref[...], preferred_element_type=F32)
    for lo in range(0, h.shape[0], TM):
        out = _ffn_rows(h[lo:lo + TM], m, n2_ref[...], w1_ref, w3_ref, w2_ref)
        ms = jnp.mean(out * out, axis=-1, keepdims=True)
        o_ref[0, lo:lo + TM, :] = out * lax.rsqrt(ms + EPS) * fg_ref[...]


def _mix_ffn_odd(h_all, hf, hb, proj, lf, lb, ng, w_out, n2g, mod, w1, w3, w2, final_g, rows):
    b, ltot, d = h_all.shape
    ff = w1.shape[1]
    tr = TM_ODD // GRID_W
    cm = lambda a: a.reshape(b, ltot // rows, rows, a.shape[-1])
    col = lambda width, c=0: pl.BlockSpec((1, GRID_W, tr, width), lambda i, t: (i, 0, t, c))
    hspec = pl.BlockSpec((1, TM_ODD, d), lambda i, t: (i, t, 0))
    return pl.pallas_call(
        _mix_ffn_odd_kernel,
        grid=(b, rows * GRID_W // TM_ODD),
        in_specs=[
            hspec, col(HGRN_W), col(HGRN_W), col(HGRN_W, OD_G), col(LRU_W), col(LRU_W), col(LRU_W, OD_GB),
            _resident((1, HGRN_W)), _resident((d, d)), _resident((1, d)),
            pl.BlockSpec((1, 6, d), lambda i, t: (i, 0, 0)),
            _resident((d, ff)), _resident((d, ff)), _resident((ff, d)), _resident((1, d)),
        ],
        out_specs=hspec,
        out_shape=jax.ShapeDtypeStruct((b, rows * GRID_W, d), F32),
        compiler_params=pltpu.CompilerParams(vmem_limit_bytes=VMEM_LIMIT),
        name="mix_ffn_odd",
    )(h_all, cm(hf), cm(hb), cm(proj), cm(lf), cm(lb), cm(proj), ng, w_out, n2g, mod, w1, w3, w2, final_g)


def _pack_even_in_w(w):
    r, k, v, wd, ad, gd, q, mk, mv, mo, ip, fp = jnp.split(
        w, [512, 1024, 1536, 1664, 1792, 1920, 2432, 2944, 3456, 3968, 3976], axis=1)
    lora = jnp.concatenate([wd[:, :LORA], ad[:, :LORA], wd[:, LORA:], ad[:, LORA:]], axis=1)
    gates = jnp.concatenate([ip, fp, jnp.zeros((w.shape[0], 128 - 16), w.dtype)], axis=1)
    return jnp.concatenate([r, k, v, q, mk, mv, mo, lora, gd, gates], axis=1).astype(BF16)


def _pack_rwkv_params(mu, w0, w2, a0, a2, kk, ka, rk, lnw, lnb):
    mu_x = mu[:, None, :3 * RWKV_W]
    mu_lo = mu[:, None, 3 * RWKV_W:]
    p512 = jnp.stack([w0, a0, kk, ka, lnw, lnb, rk.reshape(2, RWKV_W), jnp.zeros_like(w0)], axis=1)
    z = jnp.zeros_like(w2)
    wcomb = jnp.concatenate([jnp.concatenate([w2, z], axis=2), jnp.concatenate([z, a2], axis=2)], axis=1)
    return mu_x, mu_lo, p512, wcomb


def _pack_lru_params(wa, ba, wi, bi):
    def dense(w):
        eye = jnp.eye(LRU_BLOCKS, dtype=w.dtype)
        return jnp.einsum('dhij,hg->dhigj', w, eye).reshape(2, LRU_W, LRU_W)
    wg = jnp.concatenate([dense(wa), dense(wi)], axis=2)
    bias = jnp.concatenate([ba, bi], axis=1)[:, None, :]
    return wg, bias


def kernel(x, c, ctx, c_ctx, norm1_g, norm2_g, mod_w, mod_b, ffn_w1, ffn_w3, ffn_w2, final_g, ev_in_w, ev_out_w, rwkv_mu, rwkv_w0, rwkv_w2, rwkv_a0, rwkv_a2, rwkv_kk, rwkv_ka, rwkv_rk, rwkv_lnw, rwkv_lnb, rwkv_g2, mlstm_bi, mlstm_bf, mlstm_ng, od_in_w, od_out_w, hgrn_lb, hgrn_ng, lru_conv_w, lru_conv_b, lru_wa, lru_ba, lru_wi, lru_bi, lru_lam):
    b, seq, d = x.shape
    n_ctx_tok = ctx.shape[1]
    rows = seq // GRID_W
    n_lat_tiles = seq // TM
    depth = mod_w.shape[0]

    assert depth == 2, "layer 0 reads x / ctx directly and the odd layer is the last one"
    c8 = jnp.concatenate([c, c_ctx[None, :], jnp.zeros((8 - b - 1, d), F32)], axis=0)
    mod = _modulation(c8, mod_w, mod_b).reshape(depth, 8, 6, d)

    h_all = None
    for l in range(depth):
        j = l // 2
        g1 = norm1_g[l][None, :]
        ffn_w = (ffn_w1[l].astype(BF16), ffn_w3[l].astype(BF16), ffn_w2[l].astype(BF16))
        if l % 2 == 0:
            proj = _in_proj_even(x, ctx, g1, mod[l], _pack_even_in_w(ev_in_w[j]), n_lat_tiles)
            mu_x, mu_lo, p512, wcomb = _pack_rwkv_params(
                rwkv_mu[j], rwkv_w0[j], rwkv_w2[j], rwkv_a0[j], rwkv_a2[j], rwkv_kk[j], rwkv_ka[j],
                rwkv_rk[j], rwkv_lnw[j], rwkv_lnb[j])
            rf, rb = _rwkv(proj, mu_x, mu_lo, p512, wcomb, seq // T_RWKV, n_ctx_tok // T_RWKV, bb=4)
            gbias = jnp.concatenate([mlstm_bi[j].reshape(-1), mlstm_bf[j].reshape(-1),
                                     jnp.zeros((128 - 4 * MLSTM_HEADS,), F32)])[None, :]
            mf, mb = _mlstm(proj, gbias, seq // T_MLSTM, n_ctx_tok // T_MLSTM, bb=2)
            h_all = _mix_ffn_even(x, ctx, rf, rb, proj, mf, mb, rwkv_g2[j], mlstm_ng[j][None, :],
                                  ev_out_w[j].astype(BF16), norm2_g[l][None, :], mod[l], *ffn_w, n_lat_tiles)
        else:
            proj = _in_proj_odd(h_all, g1, mod[l], od_in_w[j].astype(BF16), rows)
            hf, hb = _hgrn(proj, hgrn_lb, l, seq // T_HGRN, n_ctx_tok // T_HGRN, bb=4)
            wg, bias = _pack_lru_params(lru_wa[j], lru_ba[j], lru_wi[j], lru_bi[j])
            lf, lb = _lru(proj, lru_conv_w[j], lru_conv_b[j][None, :], wg, bias, lru_lam[j][:, None, :],
                          seq // T_LRU, n_ctx_tok // T_LRU)
            h_all = _mix_ffn_odd(h_all, hf, hb, proj, lf, lb, hgrn_ng[j][None, :], od_out_w[j].astype(BF16),
                                 norm2_g[l][None, :], mod[l], *ffn_w, final_g[None, :], rows)
    return h_all
```

```python
import functools

import jax
import jax.numpy as jnp
from jax import lax
from jax.experimental import pallas as pl
from jax.experimental.pallas import tpu as pltpu

F32 = jnp.float32
BF16 = jnp.bfloat16

D_MODEL = 1024
GRID_W = 64
D_FF = 2816
EPS = 1e-6

RWKV_HEADS = 8
RWKV_HD = 64
RWKV_W = RWKV_HEADS * RWKV_HD
LORA = 64
GATE_LORA = 128
RWKV_GN_EPS = 64e-5
MLSTM_HEADS = 4
MLSTM_HD = 128
MLSTM_W = MLSTM_HEADS * MLSTM_HD
HGRN_HEADS = 4
HGRN_FD = 128
HGRN_W = HGRN_HEADS * HGRN_FD
LRU_W = 512
LRU_BLOCKS = 8
LRU_BD = LRU_W // LRU_BLOCKS
LRU_C = 8.0
SUBLANES = 8

T_RWKV = 64
T_MLSTM = 128
T_HGRN = 64
T_LRU = 256
TM = 256
TM_ODD = 8 * GRID_W

EV_N = 4096
EV_RKV, EV_Q, EV_K, EV_V, EV_O = 0, 3, 4, 5, 6
EV_LORA, EV_GD, EV_GATES = 28, 30, 31
OD_N = 3584
OD_Q, OD_F, OD_I, OD_G, OD_XB, OD_GB = 0, 1, 3, 4, 5, 6

VMEM_LIMIT = 56 * 1024 * 1024

NT = ((1,), (1,))
NN = ((1,), (0,))
TN = ((0,), (0,))


def _split(x, n):
    parts = []
    r = x
    for i in range(n):
        p = r.astype(BF16)
        parts.append(p)
        if i + 1 < n:
            r = r - p.astype(F32)
    return parts


def _mm(a, b, dims=NN, pa=1, pb=1):
    dn = (dims, ((), ()))
    ap = _split(a, pa) if a.dtype != BF16 else [a]
    bp = _split(b, pb) if b.dtype != BF16 else [b]
    n = max(len(ap), len(bp))
    acc = None
    for i, x in enumerate(ap):
        for j, y in enumerate(bp):
            if i + j < n:
                t = lax.dot_general(x, y, dn, preferred_element_type=F32)
                acc = t if acc is None else acc + t
    return acc


def _sigmoid(x):
    return 1.0 / (1.0 + jnp.exp(-x))


def _silu(x):
    return x * _sigmoid(x)


def _log_sigmoid(x):
    return jnp.minimum(x, 0.0) - jnp.log1p(jnp.exp(-jnp.abs(x)))


def _softplus(x):
    return jnp.maximum(x, 0.0) + jnp.log1p(jnp.exp(-jnp.abs(x)))


def _masks(T, reverse):
    ti = lax.broadcasted_iota(jnp.int32, (T, T), 0)
    si = lax.broadcasted_iota(jnp.int32, (T, T), 1)
    if reverse:
        return si >= ti, si > ti
    return si <= ti, si < ti


def _ones_mask(incl):
    return jnp.where(incl, 1.0, 0.0).astype(BF16)


def _shift_prev(x, carry_row, reverse):
    T = x.shape[0]
    row = lax.broadcasted_iota(jnp.int32, x.shape, 0)
    if reverse:
        return jnp.where(row == T - 1, carry_row, pltpu.roll(x, T - 1, 0))
    return jnp.where(row == 0, carry_row, pltpu.roll(x, 1, 0))


def _fwd_blk(i, n_lat, n_ctx):
    return jnp.where(i < n_ctx, n_lat + i, i - n_ctx)


def _bwd_blk(i, n_lat, n_ctx):
    return n_lat + n_ctx - 1 - i


def _mod_kernel(c_ref, w_ref, b_ref, o_ref):
    c = c_ref[...]
    o_ref[0] = _mm(_silu(c), w_ref[0], NN, 3, 3) + b_ref[0]


def _modulation(c8, mod_w, mod_b):
    depth, d, n = mod_w.shape
    return pl.pallas_call(
        _mod_kernel,
        grid=(depth, n // d),
        in_specs=[
            pl.BlockSpec((8, d), lambda l, j: (0, 0)),
            pl.BlockSpec((1, d, d), lambda l, j: (l, 0, j)),
            pl.BlockSpec((1, 1, d), lambda l, j: (l, 0, j)),
        ],
        out_specs=pl.BlockSpec((1, 8, d), lambda l, j: (l, 0, j)),
        out_shape=jax.ShapeDtypeStruct((depth, 8, n), F32),
        compiler_params=pltpu.CompilerParams(vmem_limit_bytes=VMEM_LIMIT),
        name="adaln_modulation",
    )(c8, mod_w, mod_b.reshape(depth, 1, n))


def _norm_mod(x, g, shift, scale):
    ms = jnp.mean(x * x, axis=-1, keepdims=True)
    return (x * lax.rsqrt(ms + EPS) * g) * (1.0 + scale) + shift


def _lat_or_ctx(x_ref, c_ref, n_lat_tiles):
    return jnp.where(pl.program_id(1) < n_lat_tiles, x_ref[0], c_ref[0])


def _lat_ctx_specs(d, n_lat_tiles):
    return [
        pl.BlockSpec((1, TM, d), lambda i, t: (i, jnp.minimum(t, n_lat_tiles - 1), 0)),
        pl.BlockSpec((1, TM, d), lambda i, t: (i, 0, 0)),
    ]


def _in_even_kernel(x_ref, c_ref, g_ref, mod_ref, w_ref, o_ref, *, n_lat_tiles):
    m = mod_ref[0]
    u = _norm_mod(_lat_or_ctx(x_ref, c_ref, n_lat_tiles), g_ref[...], m[0:1], m[1:2])
    o_ref[0] = jnp.dot(u.astype(BF16), w_ref[...], preferred_element_type=F32)


def _in_proj_even(x, ctx, g, mod, w, n_lat_tiles):
    b, seq, d = x.shape
    assert ctx.shape[1] == TM
    n = w.shape[1]
    nt = n_lat_tiles + 1
    return pl.pallas_call(
        functools.partial(_in_even_kernel, n_lat_tiles=n_lat_tiles),
        grid=(b, nt),
        in_specs=_lat_ctx_specs(d, n_lat_tiles) + [
            pl.BlockSpec((1, d), lambda i, t: (0, 0)),
            pl.BlockSpec((1, 6, d), lambda i, t: (jnp.where(t >= n_lat_tiles, 4, i), 0, 0)),
            pl.BlockSpec((d, n), lambda i, t: (0, 0)),
        ],
        out_specs=pl.BlockSpec((1, TM, n), lambda i, t: (i, t, 0)),
        out_shape=jax.ShapeDtypeStruct((b, nt * TM, n), F32),
        compiler_params=pltpu.CompilerParams(vmem_limit_bytes=VMEM_LIMIT),
        name="in_proj_even",
    )(x, ctx, g, mod, w)


def _in_odd_kernel(x_ref, g_ref, mod_ref, w_ref, *rest, scatter):
    o_ref = rest[-1]
    m = mod_ref[0]
    if scatter:
        x = jnp.concatenate([x_ref[0, :, w, :] for w in range(GRID_W)], axis=0)
    else:
        x = x_ref[0]
    u = _norm_mod(x, g_ref[...], m[0:1], m[1:2])
    y = jnp.dot(u.astype(BF16), w_ref[...], preferred_element_type=F32)
    if scatter:
        o_ref[0] = y.reshape(o_ref.shape[1:])
    else:
        for j in range(o_ref.shape[1]):
            o_ref[0, j] = y[j * o_ref.shape[2]:(j + 1) * o_ref.shape[2], :]


def _in_proj_odd(h_all, g, mod, w, rows):
    b, ltot, d = h_all.shape
    n = w.shape[1]
    lat = rows * GRID_W
    n_ctx_slots = (ltot - lat) // rows
    tr = TM_ODD // GRID_W
    out_shape = jax.ShapeDtypeStruct((b, GRID_W + n_ctx_slots, rows, n), F32)
    common = [
        pl.BlockSpec((1, d), lambda i, t: (0, 0)),
        None,
        pl.BlockSpec((d, n), lambda i, t: (0, 0)),
    ]
    lat_specs = list(common)
    lat_specs[1] = pl.BlockSpec((1, 6, d), lambda i, t: (i, 0, 0))
    proj = pl.pallas_call(
        functools.partial(_in_odd_kernel, scatter=True),
        grid=(b, lat // TM_ODD),
        in_specs=[pl.BlockSpec((1, tr, GRID_W, d), lambda i, t: (i, t, 0, 0))] + lat_specs,
        out_specs=pl.BlockSpec((1, GRID_W, tr, n), lambda i, t: (i, 0, t, 0)),
        out_shape=out_shape,
        compiler_params=pltpu.CompilerParams(vmem_limit_bytes=VMEM_LIMIT),
        name="in_proj_odd",
    )(h_all.reshape(b, ltot // GRID_W, GRID_W, d), g, mod, w)
    ctx_tok = ltot - lat
    ctx_specs = list(common)
    ctx_specs[1] = pl.BlockSpec((1, 6, d), lambda i, t: (4, 0, 0))
    return pl.pallas_call(
        functools.partial(_in_odd_kernel, scatter=False),
        grid=(b, 1),
        in_specs=[pl.BlockSpec((1, ctx_tok, d), lambda i, t: (i, lat // ctx_tok, 0))] + ctx_specs
        + [pl.BlockSpec(memory_space=pl.ANY)],
        out_specs=pl.BlockSpec((1, n_ctx_slots, rows, n), lambda i, t: (i, GRID_W // n_ctx_slots, 0, 0)),
        out_shape=out_shape,
        input_output_aliases={4: 0},
        compiler_params=pltpu.CompilerParams(vmem_limit_bytes=VMEM_LIMIT),
        name="in_proj_odd_ctx",
    )(h_all, g, mod, w, proj).reshape(b, ltot, n)


def _head_sums(parts, ones_blk):
    T = parts[0].shape[0]
    s = _mm(jnp.concatenate(parts, axis=0), ones_blk, NN, 1, 1)
    return [s[i * T:(i + 1) * T] for i in range(len(parts))]


def _rwkv_shift(x, lo, carry_x, carry_lo, mu_x, mu_lo, reverse):
    xs = x + mu_x * (_shift_prev(x, carry_x, reverse) - x)
    los = lo + mu_lo * (_shift_prev(lo, carry_lo, reverse) - lo)
    lane = lax.broadcasted_iota(jnp.int32, los.shape, 1)
    return xs, jnp.where(lane < LORA, jnp.tanh(los), los)


def _rwkv_operands(xs, wa, ss, p512, reverse):
    T = xs.shape[0]
    incl, _ = _masks(T, reverse)
    edge = 0 if reverse else T - 1
    r = xs[:, :RWKV_W]
    k = xs[:, RWKV_W:2 * RWKV_W]
    v = xs[:, 2 * RWKV_W:]
    w0, a0, k_k, k_a = (p512[i:i + 1, :] for i in range(4))
    w = -_softplus(-(w0 + wa[:, :RWKV_W])) - 0.5
    a = _sigmoid(a0 + wa[:, RWKV_W:])
    logdec = -jnp.exp(w)
    kk = (k * k_k) / jnp.maximum(jnp.sqrt(ss), 1e-12)
    k2 = k * (1.0 + (a - 1.0) * k_a)
    cl = _mm(_ones_mask(incl), logdec, NN, 1, 3)
    e_pos = jnp.exp(cl)
    e_neg = jnp.exp(-cl)
    return dict(
        A=kk * jnp.exp(cl - logdec), B=kk * a * e_neg, K=k2 * e_neg, R=r * e_pos, V=v,
        gT=e_pos[edge:edge + 1, :], r=r, k2=k2)


def _rwkv_chunks(units, states, T, p):
    mm = functools.partial(_mm, pa=p, pb=p)
    pk = (lambda t: t.astype(BF16)) if p == 1 else (lambda t: t)
    hd = RWKV_HD
    pw = 2 * hd

    def bd(x):
        lo = lax.broadcasted_iota(jnp.int32, x.shape, 1) < hd
        return pk(jnp.concatenate([jnp.where(lo, x, 0.0), jnp.where(lo, 0.0, x)], axis=0))

    jobs = []
    for ui, (u, reverse) in enumerate(units):
        for pi in range(RWKV_HEADS // 2):
            sl = slice(pi * pw, (pi + 1) * pw)
            jobs.append(dict(
                ui=ui, reverse=reverse,
                AR=pk(jnp.concatenate([u['A'][:, sl], u['R'][:, sl]], axis=0)),
                B=u['B'][:, sl], K=u['K'][:, sl], V=u['V'][:, sl], gT=u['gT'][:, sl],
                S=states[ui][pi]))
    ti = lax.broadcasted_iota(jnp.int32, (2 * T, pw), 0)
    si = lax.broadcasted_iota(jnp.int32, (2 * T, pw), 1) & (hd - 1)
    t1 = lax.broadcasted_iota(jnp.int32, (T, pw), 0)
    s1 = lax.broadcasted_iota(jnp.int32, (T, pw), 1) & (hd - 1)
    masks = {}
    for reverse in (False, True):
        if reverse:
            masks[reverse] = (si > jnp.where(ti < T, ti, ti - T - 1), s1 > t1, s1 >= t1)
        else:
            masks[reverse] = (si < jnp.where(ti < T, ti, ti - T + 1), s1 < t1, s1 <= t1)
    for j in jobs:
        g = mm(j['AR'], jnp.concatenate([bd(j['B']), bd(j['K']), bd(j['S'])], axis=0), NT)
        j['GB'], j['GK'], j['SS'] = g[:, :pw], g[:, pw:2 * pw], g[:, 2 * pw:]
    for j in jobs:
        both, strict, _ = masks[j['reverse']]
        j['LV'] = mm(jnp.where(both, j['GK'], 0.0), bd(j['V']))
        j['N'] = jnp.where(strict, j['GB'][:T], 0.0)
    rc = t1 ^ s1
    eye = jnp.where(rc == 0, 1.0, 0.0)
    for j in jobs:
        j['D'] = eye - jnp.where(rc == 1, j['N'], 0.0)
    s = 2
    while s < T:
        level = (rc >= s) & (rc < 2 * s)
        for j in jobs:
            j['LD'] = mm(jnp.where(level, j['N'], 0.0), bd(j['D']))
        for j in jobs:
            j['D'] = j['D'] - mm(j['D'], bd(j['LD']))
        s *= 2
    for j in jobs:
        j['X'] = mm(j['D'], bd(j['SS'][:T] + j['LV'][:T]))
    for j in jobs:
        incl = masks[j['reverse']][2]
        j['O'] = j['SS'][T:] + j['LV'][T:] - mm(jnp.where(incl, j['GB'][T:], 0.0), bd(j['X']))
    lo = lax.broadcasted_iota(jnp.int32, (hd, pw), 1) < hd
    for j in jobs:
        full = mm(jnp.concatenate([j['V'], -j['X']], axis=0), jnp.concatenate([j['K'], j['B']], axis=0), TN)
        j['S_new'] = (j['S'] + jnp.where(lo, full[:hd], full[hd:])) * j['gT']
    outs, new_states = [], []
    for ui in range(len(units)):
        mine = [j for j in jobs if j['ui'] == ui]
        outs.append(jnp.concatenate([j['O'] for j in mine], axis=1))
        new_states.append([j['S_new'] for j in mine])
    return outs, new_states


def _rwkv_kernel(xf_ref, xb_ref, lf_ref, lb_ref, mux_ref, mulo_ref, p512_ref, wcomb_ref, ones_ref,
                 of_ref, ob_ref, s_ref, cx_ref, cl_ref, *, n_ctx, prec):
    i = pl.program_id(1)
    T = xf_ref.shape[1]

    @pl.when(i == 0)
    def _():
        s_ref[...] = jnp.zeros_like(s_ref)

    @pl.when((i == 0) | (i == n_ctx))
    def _():
        cx_ref[...] = jnp.zeros_like(cx_ref)
        cl_ref[...] = jnp.zeros_like(cl_ref)

    ones_blk = ones_ref[...]
    refs = ((xf_ref, lf_ref, of_ref), (xb_ref, lb_ref, ob_ref))
    nb = xf_ref.shape[0]
    where = [(bi, d) for d in range(2) for bi in range(nb)]
    xs, zs = [], []
    for bi, d in where:
        x = refs[d][0][bi]
        lo = refs[d][1][bi]
        edge = 0 if d else T - 1
        xs_u, z_u = _rwkv_shift(x, lo, cx_ref[bi, d, 0:1, :], cl_ref[bi, d, 0:1, :], mux_ref[d],
                                mulo_ref[d], bool(d))
        cx_ref[bi, d, 0:1, :] = x[edge:edge + 1, :]
        cl_ref[bi, d, 0:1, :] = lo[edge:edge + 1, :]
        xs.append(xs_u)
        zs.append(z_u)
    was = []
    for d in range(2):
        wa = _mm(jnp.concatenate(zs[d * nb:(d + 1) * nb], axis=0), wcomb_ref[d], NN, 2, 2)
        was += [wa[bi * T:(bi + 1) * T] for bi in range(nb)]
    kk0 = [x_u[:, RWKV_W:2 * RWKV_W] * p512_ref[d][2:3, :] for x_u, (_, d) in zip(xs, where)]
    sss = _head_sums([k0 * k0 for k0 in kk0], ones_blk)
    units = [(_rwkv_operands(x_u, wa, ss, p512_ref[d], bool(d)), bool(d))
             for x_u, wa, ss, (_, d) in zip(xs, was, sss, where)]
    states = [[s_ref[bi, d, h] for h in range(RWKV_HEADS // 2)] for bi, d in where]
    outs, new_states = _rwkv_chunks(units, states, T, prec)
    for (bi, d), ns in zip(where, new_states):
        for h in range(RWKV_HEADS // 2):
            s_ref[bi, d, h] = ns[h]
    inv_n = 1.0 / RWKV_HD
    means = _head_sums(outs, ones_blk)
    ocs = [o - mean * inv_n for o, mean in zip(outs, means)]
    varis = _head_sums([oc * oc for oc in ocs], ones_blk)
    rks = _head_sums([u['r'] * u['k2'] * p512_ref[d][6:7, :] for (u, _), (_, d) in zip(units, where)], ones_blk)
    for (bi, d), (u, _), oc, var, rk in zip(where, units, ocs, varis, rks):
        ln_w, ln_b = p512_ref[d][4:5, :], p512_ref[d][5:6, :]
        gn = oc * lax.rsqrt(var * inv_n + RWKV_GN_EPS) * ln_w + ln_b
        refs[d][2][bi] = gn + rk * u['V']


def _rwkv(proj, mu_x, mu_lo, p512, wcomb, n_lat, n_ctx, prec=1, bb=1):
    b, ltot, _ = proj.shape
    T = T_RWKV
    ones_blk = jnp.kron(jnp.eye(RWKV_HEADS, dtype=F32), jnp.ones((RWKV_HD, RWKV_HD), F32)).astype(BF16)
    fwd = functools.partial(_fwd_blk, n_lat=n_lat, n_ctx=n_ctx)
    bwd = functools.partial(_bwd_blk, n_lat=n_lat, n_ctx=n_ctx)
    full = lambda shape: pl.BlockSpec(shape, lambda bi, i: (0,) * len(shape))
    out = jax.ShapeDtypeStruct((b, ltot, RWKV_W), F32)
    return pl.pallas_call(
        functools.partial(_rwkv_kernel, n_ctx=n_ctx, prec=prec),
        grid=(b // bb, n_lat + n_ctx),
        in_specs=[
            pl.BlockSpec((bb, T, 3 * RWKV_W), lambda bi, i: (bi, fwd(i), EV_RKV)),
            pl.BlockSpec((bb, T, 3 * RWKV_W), lambda bi, i: (bi, bwd(i), EV_RKV)),
            pl.BlockSpec((bb, T, 128), lambda bi, i: (bi, fwd(i), EV_LORA)),
            pl.BlockSpec((bb, T, 128), lambda bi, i: (bi, bwd(i), EV_LORA + 1)),
            full((2, 1, 3 * RWKV_W)), full((2, 1, 128)), full((2, 8, RWKV_W)),
            full((2, 128, 2 * RWKV_W)), full((RWKV_W, RWKV_W)),
        ],
        out_specs=[
            pl.BlockSpec((bb, T, RWKV_W), lambda bi, i: (bi, fwd(i), 0)),
            pl.BlockSpec((bb, T, RWKV_W), lambda bi, i: (bi, bwd(i), 0)),
        ],
        out_shape=[out, out],
        scratch_shapes=[
            pltpu.VMEM((bb, 2, RWKV_HEADS // 2, RWKV_HD, 2 * RWKV_HD), F32),
            pltpu.VMEM((bb, 2, 8, 3 * RWKV_W), F32),
            pltpu.VMEM((bb, 2, 8, 128), F32),
        ],
        compiler_params=pltpu.CompilerParams(
            dimension_semantics=("arbitrary", "arbitrary"), vmem_limit_bytes=VMEM_LIMIT),
        name="rwkv7_chunked",
    )(proj, proj, proj, proj, mu_x, mu_lo, p512, wcomb, ones_blk)


def _mlstm_chunks(units, gbias, c_ref, n_ref, m_ref, p):
    mm = functools.partial(_mm, pa=p, pb=p)
    pk = (lambda t: t.astype(BF16)) if p == 1 else (lambda t: t)
    jobs = []
    for base, d, q, k, v, gates, reverse in units:
        T = q.shape[0]
        incl, _ = _masks(T, reverse)
        edge = 0 if reverse else T - 1
        g = gates + gbias
        lane = lax.broadcasted_iota(jnp.int32, g.shape, 1)
        g = jnp.where(lane < 2 * MLSTM_HEADS, g, _log_sigmoid(g))
        gt = g.T
        ones = _ones_mask(incl)
        b_col = _mm(ones, g, NN, 1, 3)
        b_row = _mm(gt, ones, NT, 3, 1)
        for h in range(MLSTM_HEADS):
            sl = slice(h * MLSTM_HD, (h + 1) * MLSTM_HD)
            ci = d * MLSTM_HEADS + h
            cf = 2 * MLSTM_HEADS + ci
            st = base + h
            jobs.append(dict(
                st=st, incl=incl, edge=edge,
                q=pk(q[:, sl] * (MLSTM_HD ** -0.5)), k=k[:, sl], kp=pk(k[:, sl]), v=v[:, sl],
                bc=b_col[:, cf:cf + 1], br=b_row[cf:cf + 1, :], li_c=g[:, ci:ci + 1], li_r=gt[ci:ci + 1, :],
                C0=c_ref[st], n0=n_ref[st, 0:1, :], m0=m_ref[st, 0:1, 0:1]))
    for j in jobs:
        rhs = jnp.concatenate([j['kp'], pk(j['C0']), pk(jnp.broadcast_to(j['n0'], (MLSTM_HD, MLSTM_HD)))], axis=0)
        g = mm(j['q'], rhs, NT)
        T = j['kp'].shape[0]
        j['qk'], j['qC'], j['qn'] = g[:, :T], g[:, T:T + MLSTM_HD], g[:, T + MLSTM_HD:]
    for j in jobs:
        logD = jnp.where(j['incl'], j['bc'] - j['br'] + j['li_r'], -jnp.inf)
        m_prev = j['bc'] + j['m0']
        j['m'] = jnp.maximum(m_prev, jnp.max(logD, axis=-1, keepdims=True))
        j['s'] = pk(j['qk'] * jnp.exp(logD - j['m']))
        j['inter'] = jnp.exp(m_prev - j['m'])
        bT = j['bc'][j['edge']:j['edge'] + 1, :]
        gsum = bT - j['bc'] + j['li_c']
        j['m_new'] = jnp.maximum(bT + j['m0'], jnp.max(gsum, axis=0, keepdims=True))
        j['wts'] = jnp.exp(gsum - j['m_new'])
        j['cd'] = jnp.exp(bT + j['m0'] - j['m_new'])
    ones_tt = jnp.ones((jobs[0]['s'].shape[1], MLSTM_HD), F32)
    for j in jobs:
        g = mm(j['s'], pk(jnp.concatenate([j['v'], ones_tt], axis=1)))
        j['sv'], j['rs'] = g[:, :MLSTM_HD], g[:, MLSTM_HD:]
    for j in jobs:
        j['dC'] = mm(j['wts'] * j['v'], j['kp'], TN)
    outs = []
    for j in jobs:
        num = j['sv'] + j['inter'] * j['qC']
        den = j['rs'] + j['inter'] * j['qn']
        outs.append(num / jnp.maximum(jnp.abs(den), jnp.exp(-j['m'])))
        st = j['st']
        c_ref[st] = j['cd'] * j['C0'] + j['dC']
        n_ref[st, 0:1, :] = j['cd'] * j['n0'] + jnp.sum(j['wts'] * j['k'], axis=0, keepdims=True)
        m_ref[st] = jnp.broadcast_to(j['m_new'], m_ref.shape[1:])
    return [jnp.concatenate(outs[u * MLSTM_HEADS:(u + 1) * MLSTM_HEADS], axis=1) for u in range(len(units))]


def _mlstm_kernel(qf_ref, kf_ref, vf_ref, gf_ref, qb_ref, kb_ref, vb_ref, gb_ref, gbias_ref,
                  of_ref, ob_ref, c_ref, n_ref, m_ref, *, prec):
    i = pl.program_id(1)

    @pl.when(i == 0)
    def _():
        c_ref[...] = jnp.zeros_like(c_ref)
        n_ref[...] = jnp.zeros_like(n_ref)
        m_ref[...] = jnp.zeros_like(m_ref)

    refs = ((qf_ref, kf_ref, vf_ref, gf_ref, of_ref), (qb_ref, kb_ref, vb_ref, gb_ref, ob_ref))
    units, where = [], []
    for bi in range(qf_ref.shape[0]):
        for d, (q_ref, k_ref, v_ref, g_ref, o_ref) in enumerate(refs):
            units.append(((bi * 2 + d) * MLSTM_HEADS, d, q_ref[bi], k_ref[bi], v_ref[bi], g_ref[bi], bool(d)))
            where.append((o_ref, bi))
    outs = _mlstm_chunks(units, gbias_ref[...], c_ref, n_ref, m_ref, prec)
    for (o_ref, bi), o in zip(where, outs):
        o_ref[bi] = o


def _mlstm(proj, gbias, n_lat, n_ctx, prec=1, bb=1):
    b, ltot, _ = proj.shape
    T = T_MLSTM
    fwd = functools.partial(_fwd_blk, n_lat=n_lat, n_ctx=n_ctx)
    bwd = functools.partial(_bwd_blk, n_lat=n_lat, n_ctx=n_ctx)
    spec = lambda width, col, order: pl.BlockSpec((bb, T, width), lambda bi, i: (bi, order(i), col))
    out = jax.ShapeDtypeStruct((b, ltot, MLSTM_W), F32)
    nst = bb * 2 * MLSTM_HEADS
    return pl.pallas_call(
        functools.partial(_mlstm_kernel, prec=prec),
        grid=(b // bb, n_lat + n_ctx),
        in_specs=[
            spec(MLSTM_W, EV_Q, fwd), spec(MLSTM_W, EV_K, fwd), spec(MLSTM_W, EV_V, fwd),
            spec(128, EV_GATES, fwd),
            spec(MLSTM_W, EV_Q, bwd), spec(MLSTM_W, EV_K, bwd), spec(MLSTM_W, EV_V, bwd),
            spec(128, EV_GATES, bwd),
            pl.BlockSpec((1, 128), lambda bi, i: (0, 0)),
        ],
        out_specs=[
            pl.BlockSpec((bb, T, MLSTM_W), lambda bi, i: (bi, fwd(i), 0)),
            pl.BlockSpec((bb, T, MLSTM_W), lambda bi, i: (bi, bwd(i), 0)),
        ],
        out_shape=[out, out],
        scratch_shapes=[
            pltpu.VMEM((nst, MLSTM_HD, MLSTM_HD), F32),
            pltpu.VMEM((nst, 8, MLSTM_HD), F32),
            pltpu.VMEM((nst, 8, 128), F32),
        ],
        compiler_params=pltpu.CompilerParams(
            dimension_semantics=("arbitrary", "arbitrary"), vmem_limit_bytes=VMEM_LIMIT),
        name="mlstm_chunked",
    )(proj, proj, proj, proj, proj, proj, proj, proj, gbias)


def _hgrn_chunks(units, lb, s_ref, p):
    mm = functools.partial(_mm, pa=p, pb=p)
    jobs = []
    for base, q, fpre, vi, reverse in units:
        T = q.shape[0]
        incl, _ = _masks(T, reverse)
        edge = 0 if reverse else T - 1
        f = lb + (1.0 - lb) * _sigmoid(fpre)
        kk = 1.0 - f
        qs = _silu(q)
        bc = _mm(_ones_mask(incl), jnp.log(f), NN, 1, 3)
        bmid = bc[T // 2:T // 2 + 1, :]
        bT = bc[edge:edge + 1, :]
        qt = qs * jnp.exp(bc - bmid)
        kt = kk * jnp.exp(bmid - bc)
        qd = qs * jnp.exp(bc)
        kd = kk * jnp.exp(bT - bc)
        eT = jnp.exp(bT)
        for h in range(HGRN_HEADS):
            sl = slice(h * HGRN_FD, (h + 1) * HGRN_FD)
            jobs.append(dict(st=base + h, incl=incl, qt=qt[:, sl], kt=kt[:, sl], qd=qd[:, sl], kd=kd[:, sl],
                             v=vi[:, sl], eT=eT[:, sl], S=s_ref[base + h]))
    for j in jobs:
        j['A'] = jnp.where(j['incl'], mm(j['qt'], j['kt'], NT), 0.0)
    for j in jobs:
        j['qS'] = mm(j['qd'], j['S'], NT)
    for j in jobs:
        j['o'] = mm(j['A'], j['v']) + j['qS']
    for j in jobs:
        s_ref[j['st']] = j['S'] * j['eT'] + mm(j['v'], j['kd'], TN)
    outs = [j['o'] for j in jobs]
    return [jnp.concatenate(outs[u * HGRN_HEADS:(u + 1) * HGRN_HEADS], axis=1) for u in range(len(units))]


def _hgrn_kernel(qf_ref, ff_ref, if_ref, qb_ref, fb_ref, ib_ref, lb_ref, of_ref, ob_ref, s_ref,
                 *, layer, prec):
    i = pl.program_id(1)

    @pl.when(i == 0)
    def _():
        s_ref[...] = jnp.zeros_like(s_ref)

    raw = lb_ref[...]
    e = jnp.exp(raw - jnp.max(raw, axis=0, keepdims=True))
    sm = e / jnp.sum(e, axis=0, keepdims=True)
    lb = jnp.sum(sm[:layer + 1], axis=0, keepdims=True) - sm[0:1]
    refs = ((qf_ref, ff_ref, if_ref, of_ref), (qb_ref, fb_ref, ib_ref, ob_ref))
    units, where = [], []
    for bi in range(qf_ref.shape[0]):
        for d, (q_ref, f_ref, v_ref, o_ref) in enumerate(refs):
            units.append(((bi * 2 + d) * HGRN_HEADS, q_ref[bi], f_ref[bi], v_ref[bi], bool(d)))
            where.append((o_ref, bi))
    outs = _hgrn_chunks(units, lb, s_ref, prec)
    for (o_ref, bi), o in zip(where, outs):
        o_ref[bi] = o


def _hgrn(proj, hgrn_lb, layer, n_lat, n_ctx, prec=1, bb=1):
    b, ltot, _ = proj.shape
    T = T_HGRN
    fwd = functools.partial(_fwd_blk, n_lat=n_lat, n_ctx=n_ctx)
    bwd = functools.partial(_bwd_blk, n_lat=n_lat, n_ctx=n_ctx)
    spec = lambda col, order: pl.BlockSpec((bb, T, HGRN_W), lambda bi, i: (bi, order(i), col))
    out = jax.ShapeDtypeStruct((b, ltot, HGRN_W), F32)
    return pl.pallas_call(
        functools.partial(_hgrn_kernel, layer=layer, prec=prec),
        grid=(b // bb, n_lat + n_ctx),
        in_specs=[
            spec(OD_Q, fwd), spec(OD_F, fwd), spec(OD_I, fwd),
            spec(OD_Q, bwd), spec(OD_F + 1, bwd), spec(OD_I, bwd),
            pl.BlockSpec(hgrn_lb.shape, lambda bi, i: (0, 0)),
        ],
        out_specs=[
            pl.BlockSpec((bb, T, HGRN_W), lambda bi, i: (bi, fwd(i), 0)),
            pl.BlockSpec((bb, T, HGRN_W), lambda bi, i: (bi, bwd(i), 0)),
        ],
        out_shape=[out, out],
        scratch_shapes=[pltpu.VMEM((bb * 2 * HGRN_HEADS, HGRN_FD, HGRN_FD), F32)],
        compiler_params=pltpu.CompilerParams(
            dimension_semantics=("arbitrary", "arbitrary"), vmem_limit_bytes=VMEM_LIMIT),
        name="hgrn2_chunked",
    )(proj, proj, proj, proj, proj, proj, hgrn_lb)


def _lru_conv(x, halo_prev, halo_next, has_prev, has_next, cw, cb):
    T = x.shape[0]
    row = lax.broadcasted_iota(jnp.int32, x.shape, 0)
    hp = jnp.where(has_prev, halo_prev, 0.0)
    hn = jnp.where(has_next, halo_next, 0.0)
    x_m1 = jnp.where(row == 0, hp[7:8, :], pltpu.roll(x, 1, 0))
    x_m2 = jnp.where(row == 0, hp[6:7, :], jnp.where(row == 1, hp[7:8, :], pltpu.roll(x, 2, 0)))
    x_p1 = jnp.where(row == T - 1, hn[0:1, :], pltpu.roll(x, T - 1, 0))
    return x_m2 * cw[0:1, :] + x_m1 * cw[1:2, :] + x * cw[2:3, :] + x_p1 * cw[3:4, :] + cb


def _lru_dir(d, xc, wg, bias, sp_lam, h_ref, reverse, p):
    T = xc.shape[0]
    gates = _sigmoid(_mm(xc, wg, NN, p, p) + bias)
    r = gates[:, :LRU_W]
    ig = gates[:, LRU_W:]
    log_a = -LRU_C * r * sp_lam
    a = jnp.exp(log_a)
    u = jnp.sqrt((1.0 - a) * (1.0 + a)) * (ig * xc)
    sub = lax.broadcasted_iota(jnp.int32, xc.shape, 0) & (SUBLANES - 1)
    k = 1
    while k < SUBLANES:
        ok = (sub < SUBLANES - k) if reverse else (sub >= k)
        shift = T - k if reverse else k
        a_n = jnp.where(ok, pltpu.roll(a, shift, 0), 1.0)
        u_n = jnp.where(ok, pltpu.roll(u, shift, 0), 0.0)
        u = a * u_n + u
        a = a * a_n
        k *= 2
    n_groups = T // SUBLANES
    last = 0 if reverse else SUBLANES - 1
    carry = jnp.broadcast_to(h_ref[d, 0:1, :], (SUBLANES, LRU_W))
    pieces = [None] * n_groups
    for g in (range(n_groups - 1, -1, -1) if reverse else range(n_groups)):
        rows = slice(g * SUBLANES, (g + 1) * SUBLANES)
        pieces[g] = a[rows] * carry + u[rows]
        carry = jnp.broadcast_to(pieces[g][last:last + 1, :], (SUBLANES, LRU_W))
    h_ref[d, 0:1, :] = carry[0:1, :]
    return jnp.concatenate(pieces, axis=0)


def _lru_kernel(xf_ref, pf_ref, nf_ref, xb_ref, pb_ref, nb_ref, cw_ref, cb_ref, wg_ref, bias_ref,
                lam_ref, of_ref, ob_ref, h_ref, *, n_lat, n_ctx, prec):
    i = pl.program_id(1)

    @pl.when(i == 0)
    def _():
        h_ref[...] = jnp.zeros_like(h_ref)

    cw = cw_ref[...]
    cb = cb_ref[...]
    for d, (x_ref, p_ref, n_ref, o_ref) in enumerate(
            ((xf_ref, pf_ref, nf_ref, of_ref), (xb_ref, pb_ref, nb_ref, ob_ref))):
        blk = _bwd_blk(i, n_lat, n_ctx) if d else _fwd_blk(i, n_lat, n_ctx)
        has_prev = (blk != 0) & (blk != n_lat)
        has_next = (blk != n_lat - 1) & (blk != n_lat + n_ctx - 1)
        xc = _lru_conv(x_ref[0], p_ref[0], n_ref[0], has_prev, has_next, cw, cb)
        sp_lam = _softplus(-lam_ref[d])
        o_ref[0] = _lru_dir(d, xc, wg_ref[d], bias_ref[d], sp_lam, h_ref, bool(d), prec)


def _lru(proj, conv_w, conv_b, wg, bias, lam, n_lat, n_ctx, prec=1):
    b, ltot, _ = proj.shape
    T = T_LRU
    r8 = T // 8
    n8 = ltot // 8
    fwd = functools.partial(_fwd_blk, n_lat=n_lat, n_ctx=n_ctx)
    bwd = functools.partial(_bwd_blk, n_lat=n_lat, n_ctx=n_ctx)
    col8 = OD_XB
    cur = lambda order: pl.BlockSpec((1, T, LRU_W), lambda bi, i: (bi, order(i), OD_XB))
    prev = lambda order: pl.BlockSpec(
        (1, 8, LRU_W), lambda bi, i: (bi, jnp.maximum(order(i) * r8 - 1, 0), col8))
    nxt = lambda order: pl.BlockSpec(
        (1, 8, LRU_W), lambda bi, i: (bi, jnp.minimum((order(i) + 1) * r8, n8 - 1), col8))
    full = lambda shape: pl.BlockSpec(shape, lambda bi, i: (0,) * len(shape))
    out = jax.ShapeDtypeStruct((b, ltot, LRU_W), F32)
    return pl.pallas_call(
        functools.partial(_lru_kernel, n_lat=n_lat, n_ctx=n_ctx, prec=prec),
        grid=(b, n_lat + n_ctx),
        in_specs=[
            cur(fwd), prev(fwd), nxt(fwd), cur(bwd), prev(bwd), nxt(bwd),
            full((4, LRU_W)), full((1, LRU_W)), full((2, LRU_W, 2 * LRU_W)), full((2, 1, 2 * LRU_W)),
            full((2, 1, LRU_W)),
        ],
        out_specs=[
            pl.BlockSpec((1, T, LRU_W), lambda bi, i: (bi, fwd(i), 0)),
            pl.BlockSpec((1, T, LRU_W), lambda bi, i: (bi, bwd(i), 0)),
        ],
        out_shape=[out, out],
        scratch_shapes=[pltpu.VMEM((2, 8, LRU_W), F32)],
        compiler_params=pltpu.CompilerParams(
            dimension_semantics=("arbitrary", "arbitrary"), vmem_limit_bytes=VMEM_LIMIT),
        name="rglru_scan",
    )(proj, proj, proj, proj, proj, proj, conv_w, conv_b, wg, bias, lam)


def _head_rms(x, heads, width):
    outs = []
    for h in range(heads):
        xh = x[:, h * width:(h + 1) * width]
        outs.append(xh * lax.rsqrt(jnp.mean(xh * xh, axis=-1, keepdims=True) + EPS))
    return jnp.concatenate(outs, axis=1)


def _ffn_rows(h, m, g, w1_ref, w3_ref, w2_ref):
    v = _norm_mod(h, g, m[3:4], m[4:5]).astype(BF16)
    a = jnp.dot(v, w1_ref[...], preferred_element_type=F32)
    c = jnp.dot(v, w3_ref[...], preferred_element_type=F32)
    hid = (_silu(a) * c).astype(BF16)
    return h + m[5:6] * jnp.dot(hid, w2_ref[...], preferred_element_type=F32)


def _resident(shape):
    return pl.BlockSpec(shape, lambda i, t: (0,) * len(shape), pipeline_mode=pl.Buffered(1))


def _mix_ffn_even_kernel(x_ref, c_ref, rf_ref, rb_ref, gd_ref, mf_ref, mb_ref, op_ref, g2_ref, ng_ref, w_ref,
                         n2_ref, mod_ref, w1_ref, w3_ref, w2_ref, o_ref, *, n_lat_tiles):
    m = mod_ref[0]
    g = _mm(_sigmoid(gd_ref[0]), g2_ref[...], NN, 2, 2)
    y1 = (rf_ref[0] + rb_ref[0]) * g
    y2 = _head_rms(mf_ref[0] + mb_ref[0], MLSTM_HEADS, MLSTM_HD) * ng_ref[...] * _sigmoid(op_ref[0])
    y = jnp.concatenate([y1, y2], axis=1).astype(BF16)
    h = _lat_or_ctx(x_ref, c_ref, n_lat_tiles) + m[2:3] * jnp.dot(y, w_ref[...], preferred_element_type=F32)
    o_ref[0] = _ffn_rows(h, m, n2_ref[...], w1_ref, w3_ref, w2_ref)


def _mix_ffn_even(x, ctx, rf, rb, proj, mf, mb, g2, ng, w_out, n2g, mod, w1, w3, w2, n_lat_tiles):
    b, seq, d = x.shape
    ff = w1.shape[1]
    nt = n_lat_tiles + 1
    tok = lambda width, col=0: pl.BlockSpec((1, TM, width), lambda i, t: (i, t, col))
    return pl.pallas_call(
        functools.partial(_mix_ffn_even_kernel, n_lat_tiles=n_lat_tiles),
        grid=(b, nt),
        in_specs=_lat_ctx_specs(d, n_lat_tiles) + [
            tok(RWKV_W), tok(RWKV_W), tok(128, EV_GD), tok(MLSTM_W), tok(MLSTM_W), tok(MLSTM_W, EV_O),
            _resident((GATE_LORA, RWKV_W)), _resident((1, MLSTM_W)), _resident((d, d)), _resident((1, d)),
            pl.BlockSpec((1, 6, d), lambda i, t: (jnp.where(t >= n_lat_tiles, 4, i), 0, 0)),
            _resident((d, ff)), _resident((d, ff)), _resident((ff, d)),
        ],
        out_specs=tok(d),
        out_shape=jax.ShapeDtypeStruct((b, nt * TM, d), F32),
        compiler_params=pltpu.CompilerParams(vmem_limit_bytes=VMEM_LIMIT),
        name="mix_ffn_even",
    )(x, ctx, rf, rb, proj, mf, mb, proj, g2, ng, w_out, n2g, mod, w1, w3, w2)


def _gelu_tanh(x):
    return 0.5 * x * (1.0 + jnp.tanh(0.7978845608028654 * (x + 0.044715 * x * x * x)))


def _gather_rows(ref):
    return jnp.concatenate([ref[0, :, r, :] for r in range(ref.shape[2])], axis=0)


def _mix_ffn_odd_kernel(h_ref, hf_ref, hb_ref, g_ref, lf_ref, lb_ref, gb_ref, ng_ref, w_ref, n2_ref, mod_ref,
                        w1_ref, w3_ref, w2_ref, fg_ref, o_ref):
    m = mod_ref[0]
    hsum = _gather_rows(hf_ref) + _gather_rows(hb_ref)
    y1 = _head_rms(hsum, HGRN_HEADS, HGRN_FD) * ng_ref[...] * _silu(_gather_rows(g_ref))
    y2 = (_gather_rows(lf_ref) + _gather_rows(lb_ref)) * _gelu_tanh(_gather_rows(gb_ref))
    y = jnp.concatenate([y1, y2], axis=1).astype(BF16)
    h = h_ref[0] + m[2:3] * jnp.dot(y, w_ref[...], preferred_element_type=F32)
    for lo in range(0, h.shape[0], TM):
        out = _ffn_rows(h[lo:lo + TM], m, n2_ref[...], w1_ref, w3_ref, w2_ref)
        ms = jnp.mean(out * out, axis=-1, keepdims=True)
        o_ref[0, lo:lo + TM, :] = out * lax.rsqrt(ms + EPS) * fg_ref[...]


def _mix_ffn_odd(h_all, hf, hb, proj, lf, lb, ng, w_out, n2g, mod, w1, w3, w2, final_g, rows):
    b, ltot, d = h_all.shape
    ff = w1.shape[1]
    tr = TM_ODD // GRID_W
    cm = lambda a: a.reshape(b, ltot // rows, rows, a.shape[-1])
    col = lambda width, c=0: pl.BlockSpec((1, GRID_W, tr, width), lambda i, t: (i, 0, t, c))
    hspec = pl.BlockSpec((1, TM_ODD, d), lambda i, t: (i, t, 0))
    return pl.pallas_call(
        _mix_ffn_odd_kernel,
        grid=(b, rows * GRID_W // TM_ODD),
        in_specs=[
            hspec, col(HGRN_W), col(HGRN_W), col(HGRN_W, OD_G), col(LRU_W), col(LRU_W), col(LRU_W, OD_GB),
            _resident((1, HGRN_W)), _resident((d, d)), _resident((1, d)),
            pl.BlockSpec((1, 6, d), lambda i, t: (i, 0, 0)),
            _resident((d, ff)), _resident((d, ff)), _resident((ff, d)), _resident((1, d)),
        ],
        out_specs=hspec,
        out_shape=jax.ShapeDtypeStruct((b, rows * GRID_W, d), F32),
        compiler_params=pltpu.CompilerParams(vmem_limit_bytes=VMEM_LIMIT),
        name="mix_ffn_odd",
    )(h_all, cm(hf), cm(hb), cm(proj), cm(lf), cm(lb), cm(proj), ng, w_out, n2g, mod, w1, w3, w2, final_g)


def _pack_even_in_w(w):
    r, k, v, wd, ad, gd, q, mk, mv, mo, ip, fp = jnp.split(
        w, [512, 1024, 1536, 1664, 1792, 1920, 2432, 2944, 3456, 3968, 3976], axis=1)
    lora = jnp.concatenate([wd[:, :LORA], ad[:, :LORA], wd[:, LORA:], ad[:, LORA:]], axis=1)
    gates = jnp.concatenate([ip, fp, jnp.zeros((w.shape[0], 128 - 16), w.dtype)], axis=1)
    return jnp.concatenate([r, k, v, q, mk, mv, mo, lora, gd, gates], axis=1).astype(BF16)


def _pack_rwkv_params(mu, w0, w2, a0, a2, kk, ka, rk, lnw, lnb):
    mu_x = mu[:, None, :3 * RWKV_W]
    mu_lo = mu[:, None, 3 * RWKV_W:]
    p512 = jnp.stack([w0, a0, kk, ka, lnw, lnb, rk.reshape(2, RWKV_W), jnp.zeros_like(w0)], axis=1)
    z = jnp.zeros_like(w2)
    wcomb = jnp.concatenate([jnp.concatenate([w2, z], axis=2), jnp.concatenate([z, a2], axis=2)], axis=1)
    return mu_x, mu_lo, p512, wcomb


def _pack_lru_params(wa, ba, wi, bi):
    def dense(w):
        eye = jnp.eye(LRU_BLOCKS, dtype=w.dtype)
        return jnp.einsum('dhij,hg->dhigj', w, eye).reshape(2, LRU_W, LRU_W)
    wg = jnp.concatenate([dense(wa), dense(wi)], axis=2)
    bias = jnp.concatenate([ba, bi], axis=1)[:, None, :]
    return wg, bias


def kernel(x, c, ctx, c_ctx, norm1_g, norm2_g, mod_w, mod_b, ffn_w1, ffn_w3, ffn_w2, final_g, ev_in_w, ev_out_w, rwkv_mu, rwkv_w0, rwkv_w2, rwkv_a0, rwkv_a2, rwkv_kk, rwkv_ka, rwkv_rk, rwkv_lnw, rwkv_lnb, rwkv_g2, mlstm_bi, mlstm_bf, mlstm_ng, od_in_w, od_out_w, hgrn_lb, hgrn_ng, lru_conv_w, lru_conv_b, lru_wa, lru_ba, lru_wi, lru_bi, lru_lam):
    b, seq, d = x.shape
    n_ctx_tok = ctx.shape[1]
    rows = seq // GRID_W
    n_lat_tiles = seq // TM
    depth = mod_w.shape[0]

    assert depth == 2, "layer 0 reads x / ctx directly and the odd layer is the last one"
    c8 = jnp.concatenate([c, c_ctx[None, :], jnp.zeros((8 - b - 1, d), F32)], axis=0)
    mod = _modulation(c8, mod_w, mod_b).reshape(depth, 8, 6, d)

    h_all = None
    for l in range(depth):
        j = l // 2
        g1 = norm1_g[l][None, :]
        ffn_w = (ffn_w1[l].astype(BF16), ffn_w3[l].astype(BF16), ffn_w2[l].astype(BF16))
        if l % 2 == 0:
            proj = _in_proj_even(x, ctx, g1, mod[l], _pack_even_in_w(ev_in_w[j]), n_lat_tiles)
            mu_x, mu_lo, p512, wcomb = _pack_rwkv_params(
                rwkv_mu[j], rwkv_w0[j], rwkv_w2[j], rwkv_a0[j], rwkv_a2[j], rwkv_kk[j], rwkv_ka[j],
                rwkv_rk[j], rwkv_lnw[j], rwkv_lnb[j])
            rf, rb = _rwkv(proj, mu_x, mu_lo, p512, wcomb, seq // T_RWKV, n_ctx_tok // T_RWKV, bb=4)
            gbias = jnp.concatenate([mlstm_bi[j].reshape(-1), mlstm_bf[j].reshape(-1),
                                     jnp.zeros((128 - 4 * MLSTM_HEADS,), F32)])[None, :]
            mf, mb = _mlstm(proj, gbias, seq // T_MLSTM, n_ctx_tok // T_MLSTM, bb=4)
            h_all = _mix_ffn_even(x, ctx, rf, rb, proj, mf, mb, rwkv_g2[j], mlstm_ng[j][None, :],
                                  ev_out_w[j].astype(BF16), norm2_g[l][None, :], mod[l], *ffn_w, n_lat_tiles)
        else:
            proj = _in_proj_odd(h_all, g1, mod[l], od_in_w[j].astype(BF16), rows)
            hf, hb = _hgrn(proj, hgrn_lb, l, seq // T_HGRN, n_ctx_tok // T_HGRN, bb=4)
            wg, bias = _pack_lru_params(lru_wa[j], lru_ba[j], lru_wi[j], lru_bi[j])
            lf, lb = _lru(proj, lru_conv_w[j], lru_conv_b[j][None, :], wg, bias, lru_lam[j][:, None, :],
                          seq // T_LRU, n_ctx_tok // T_LRU)
            h_all = _mix_ffn_odd(h_all, hf, hb, proj, lf, lb, hgrn_ng[j][None, :], od_out_w[j].astype(BF16),
                                 norm2_g[l][None, :], mod[l], *ffn_w, final_g[None, :], rows)
    return h_all
```
